```python
import math
import jax, jax.numpy as jnp
from jax import lax
import numpy as np

D_MODEL = 1024
BATCH = 8
SEQ = 2048
DEPTH = 4
DEC_BATCH = 128
DEC_SEQ = 1
PAST_LEN = 2048
PAGE_SIZE = 128

H_A = 4
DK_A = 64
DV_A = 128
GLA_RANK = 16
GLA_GATE_NORM = 16.0
H_B = 4
DK_B = 128
DV_B = 128
H_C = 4
DH_C = 128
BRANCH_W = 512
N_BRANCH = 3
LIN_CHUNK = 64
Q_BLOCK = 128
FOX_BIAS_INIT = 4.0
ADA_SCALE = 0.5
LN_EPS = 1e-5
RMS_EPS = 1e-6
F_FLOOR = 1e-30
MASK_VALUE = -1e30
DEEPNORM_ALPHA = (2 * DEPTH) ** 0.25
DEEPNORM_BETA = (8 * DEPTH) ** -0.25
IN_WIDTHS = (H_A * DK_A, H_A * DK_A, H_A * DV_A, BRANCH_W, GLA_RANK,
             H_B * DK_B, H_B * DK_B, H_B * DV_B, BRANCH_W,
             H_C * DH_C, H_C * DH_C, H_C * DH_C, H_C, BRANCH_W,
             N_BRANCH * D_MODEL)
N_IN = sum(IN_WIDTHS)

kernel_name = 'hybrid_gla_hgrn2_fox_decode_step'


def _split_points():
    pts, acc = [], 0
    for w in IN_WIDTHS[:-1]:
        acc += w
        pts.append(acc)
    return pts


def _heads(t, n_heads):
    return t.reshape(t.shape[0], t.shape[1], n_heads, -1)


def _layernorm(x, g, b):
    xf = x.astype(jnp.float32)
    mu = jnp.mean(xf, axis=-1, keepdims=True)
    var = jnp.mean(jnp.square(xf - mu), axis=-1, keepdims=True)
    return ((xf - mu) * lax.rsqrt(var + LN_EPS) * g + b).astype(x.dtype)


def _group_rmsnorm(o, gain):
    of = o.astype(jnp.float32)
    of = of * lax.rsqrt(jnp.mean(of * of, axis=-1, keepdims=True) + RMS_EPS)
    return of.reshape(o.shape[0], o.shape[1], -1) * gain


def gated_linear_scan(q, k, v, log_decay, s0):
    bsz, seq, nh, dk = q.shape
    dv = v.shape[-1]
    chunk = math.gcd(seq, LIN_CHUNK)
    n_chunks = seq // chunk

    def to_blocks(t):
        return t.astype(jnp.float32).reshape(bsz, n_chunks, chunk, nh, t.shape[-1]).transpose(1, 0, 3, 2, 4)

    causal = jnp.tril(jnp.ones((chunk, chunk), dtype=bool))[:, :, None]

    def chunk_step(state, blk):
        q_c, k_c, v_c, g_c = blk
        b = jnp.cumsum(g_c, axis=2)
        rel = b[:, :, :, None, :] - b[:, :, None, :, :]
        decay = jnp.where(causal, jnp.exp(jnp.minimum(rel, 0.0)), 0.0)
        scores = jnp.einsum('bhtd,bhsd,bhtsd->bhts', q_c, k_c, decay)
        o_c = (jnp.einsum('bhts,bhsv->bhtv', scores, v_c)
               + jnp.einsum('bhtd,bhdv->bhtv', q_c * jnp.exp(b), state))
        b_end = b[:, :, -1:, :]
        new_state = (jnp.exp(b_end[:, :, 0, :])[..., None] * state
                     + jnp.einsum('bhsd,bhsv->bhdv', k_c * jnp.exp(jnp.minimum(b_end - b, 0.0)), v_c))
        return new_state, o_c

    s_fin, o = lax.scan(chunk_step, s0.astype(jnp.float32),
                        (to_blocks(q), to_blocks(k), to_blocks(v), to_blocks(log_decay)))
    o = o.transpose(1, 0, 3, 2, 4).reshape(bsz, seq, nh, dv)
    return o, s_fin.astype(s0.dtype)


def forgetting_attention(q, k, v, cq, ck, q_pos, k_pos):
    bsz, tq, nh, dh = q.shape
    qb = math.gcd(tq, Q_BLOCK)
    nb = tq // qb
    q_blocks = q.reshape(bsz, nb, qb, nh, dh).swapaxes(0, 1)
    cq_blocks = cq.reshape(bsz, nb, qb, nh).swapaxes(0, 1)
    pos_blocks = q_pos.reshape(nb, qb)
    ck_h = ck.transpose(0, 2, 1)
    scale = dh ** -0.5

    def attend_block(blk):
        q_i, cq_i, pos_i = blk
        logits = jnp.einsum('bthd,bshd->bhts', q_i, k).astype(jnp.float32) * scale
        logits = logits + cq_i.transpose(0, 2, 1)[..., None] - ck_h[:, :, None, :]
        visible = (k_pos[None, :] <= pos_i[:, None])[None, None]
        logits = jnp.where(visible, logits, MASK_VALUE)
        probs = jax.nn.softmax(logits, axis=-1)
        return jnp.einsum('bhts,bshd->bthd', probs.astype(v.dtype), v)

    out = lax.map(attend_block, (q_blocks, cq_blocks, pos_blocks))
    return out.swapaxes(0, 1).reshape(bsz, tq, nh, dh)


def hybrid_layer(x, c, weights, gla_s0, hgrn_s0, k_past, v_past, logf_past):
    (w_in, w_a2, b_a2, g_gla, lb, g_hgrn, b_f, w_branch, w_out, w_ada, b_ada, ln_g, ln_b) = weights
    bsz, seq, _ = x.shape
    past = k_past.shape[1]
    mod = jax.nn.silu(c) @ w_ada + b_ada
    shift, scale, gate = jnp.split(mod, 3, axis=-1)
    u = x * (1.0 + scale[:, None, :]) + shift[:, None, :]
    z = u @ w_in
    (qa, ka, va, ga, ra, qb, fb, ib, gb, qc, kc, vc, fc, gc, mg) = jnp.split(z, _split_points(), axis=-1)
    log_a = jax.nn.log_sigmoid((ra @ w_a2 + b_a2).astype(jnp.float32)) / GLA_GATE_NORM
    o_a, s_a = gated_linear_scan(_heads(qa, H_A) * (DK_A ** -0.5), _heads(ka, H_A), _heads(va, H_A),
                                 _heads(log_a, H_A), gla_s0)
    br_a = _group_rmsnorm(o_a, g_gla) * jax.nn.silu(ga)
    fb32 = fb.astype(jnp.float32)
    f_b = lb + (1.0 - lb) * jax.nn.sigmoid(fb32)
    log_f = jnp.log(jnp.maximum(f_b, F_FLOOR))
    key_b = (1.0 - lb) * jax.nn.sigmoid(-fb32)
    o_b, s_b = gated_linear_scan(_heads(jax.nn.silu(qb), H_B), _heads(key_b, H_B),
                                 _heads(ib, H_B), _heads(log_f, H_B), hgrn_s0)
    br_b = _group_rmsnorm(o_b, g_hgrn) * jax.nn.silu(gb)
    logf_c = jax.nn.log_sigmoid((fc + b_f).astype(jnp.float32))
    k_new = _heads(kc, H_C)
    v_new = _heads(vc, H_C)
    k_all = jnp.concatenate([k_past.astype(k_new.dtype), k_new], axis=1)
    v_all = jnp.concatenate([v_past.astype(v_new.dtype), v_new], axis=1)
    cum = jnp.cumsum(jnp.concatenate([logf_past.astype(jnp.float32), logf_c], axis=1), axis=1)
    o_c = forgetting_attention(_heads(qc, H_C), k_all, v_all, cum[:, past:], cum,
                               past + jnp.arange(seq), jnp.arange(past + seq))
    br_c = o_c.reshape(bsz, seq, BRANCH_W) * jax.nn.silu(gc)
    branches = jnp.stack([br_a, br_b, br_c], axis=2)
    proj = jnp.einsum('blnw,nwd->blnd', branches, w_branch)
    merged = jnp.sum(jax.nn.sigmoid(mg).reshape(bsz, seq, N_BRANCH, D_MODEL) * proj, axis=2)
    y = merged @ w_out
    x_new = _layernorm(DEEPNORM_ALPHA * x + gate[:, None, :] * y, ln_g, ln_b)
    return x_new, s_a, s_b, k_new, v_new, logf_c


def setup_inputs(seed: int = 0) -> dict:
    key = jax.random.key(seed)
    ks = jax.random.split(key, 24)
    n_pages = PAST_LEN // PAGE_SIZE
    n_used = DEC_BATCH * n_pages
    n_phys = n_used + max(1, n_used // 4)
    f32 = jnp.float32
    nrm = jax.random.normal
    page_table = jax.random.permutation(ks[0], n_phys)[:n_used].reshape(DEC_BATCH, n_pages).astype(jnp.int32)
    return {
        'x_prompt': nrm(ks[1], (BATCH, SEQ, D_MODEL), f32),
        'x_sample': nrm(ks[2], (DEC_BATCH, DEC_SEQ, D_MODEL), f32),
        'cache_fox_k': nrm(ks[3], (DEPTH, n_phys, PAGE_SIZE, H_C, DH_C), f32),
        'cache_fox_v': nrm(ks[4], (DEPTH, n_phys, PAGE_SIZE, H_C, DH_C), f32),
        'cache_fox_logf': jax.nn.log_sigmoid(FOX_BIAS_INIT + nrm(ks[5], (DEPTH, n_phys, PAGE_SIZE, H_C), f32)),
        'state_gla': 0.5 * nrm(ks[6], (DEPTH, DEC_BATCH, H_A, DK_A, DV_A), f32),
        'state_hgrn': 0.5 * nrm(ks[7], (DEPTH, DEC_BATCH, H_B, DK_B, DV_B), f32),
        'page_table': page_table,
        'c_prompt': nrm(ks[8], (BATCH, D_MODEL), f32),
        'c_sample': nrm(ks[9], (DEC_BATCH, D_MODEL), f32),
        'w_in': nrm(ks[10], (DEPTH, D_MODEL, N_IN), f32) * D_MODEL ** -0.5,
        'w_gla_a2': nrm(ks[11], (DEPTH, GLA_RANK, H_A * DK_A), f32) * GLA_RANK ** -0.5,
        'b_gla_a2': 0.1 * nrm(ks[12], (DEPTH, H_A * DK_A), f32),
        'norm_gla': 1.0 + 0.02 * nrm(ks[13], (DEPTH, H_A * DV_A), f32),
        'hgrn_lb_logits': 0.5 * nrm(ks[14], (DEPTH, H_B * DK_B), f32),
        'norm_hgrn': 1.0 + 0.02 * nrm(ks[15], (DEPTH, H_B * DV_B), f32),
        'b_fox_f': FOX_BIAS_INIT + 0.1 * nrm(ks[16], (DEPTH, H_C), f32),
        'w_branch': nrm(ks[17], (DEPTH, N_BRANCH, BRANCH_W, D_MODEL), f32) * (BRANCH_W ** -0.5) * DEEPNORM_BETA,
        'w_out': nrm(ks[18], (DEPTH, D_MODEL, D_MODEL), f32) * (D_MODEL ** -0.5) * DEEPNORM_BETA,
        'w_ada': nrm(ks[19], (DEPTH, D_MODEL, 3 * D_MODEL), f32) * (D_MODEL ** -0.5) * ADA_SCALE,
        'b_ada': 0.02 * nrm(ks[20], (DEPTH, 3 * D_MODEL), f32),
        'ln_g': 1.0 + 0.02 * nrm(ks[21], (DEPTH, D_MODEL), f32),
        'ln_b': 0.02 * nrm(ks[22], (DEPTH, D_MODEL), f32),
    }


def reference(x_prompt, x_sample, cache_fox_k, cache_fox_v, cache_fox_logf, state_gla, state_hgrn,
              page_table, c_prompt, c_sample, w_in, w_gla_a2, b_gla_a2, norm_gla, hgrn_lb_logits,
              norm_hgrn, b_fox_f, w_branch, w_out, w_ada, b_ada, ln_g, ln_b):
    bsz = x_prompt.shape[0]
    dec_bsz = x_sample.shape[0]
    past_len = page_table.shape[1] * PAGE_SIZE
    lb_p = jax.nn.softmax(hgrn_lb_logits.astype(jnp.float32), axis=0)
    lower_bounds = jnp.clip(jnp.cumsum(lb_p, axis=0) - lb_p[0:1], 0.0, 1.0)
    xp, xs = x_prompt, x_sample
    gla_p, gla_s, hg_p, hg_s = [], [], [], []
    kp, vp, fp, ksm, vsm, fsm = [], [], [], [], [], []
    for l in range(DEPTH):
        weights = (w_in[l], w_gla_a2[l], b_gla_a2[l], norm_gla[l], lower_bounds[l], norm_hgrn[l],
                   b_fox_f[l], w_branch[l], w_out[l], w_ada[l], b_ada[l], ln_g[l], ln_b[l])
        xp, sa, sb, k_new, v_new, f_new = hybrid_layer(
            xp, c_prompt, weights,
            jnp.zeros((bsz, H_A, DK_A, DV_A), jnp.float32),
            jnp.zeros((bsz, H_B, DK_B, DV_B), jnp.float32),
            jnp.zeros((bsz, 0, H_C, DH_C), x_prompt.dtype),
            jnp.zeros((bsz, 0, H_C, DH_C), x_prompt.dtype),
            jnp.zeros((bsz, 0, H_C), jnp.float32))
        gla_p.append(sa); hg_p.append(sb); kp.append(k_new); vp.append(v_new); fp.append(f_new)
        k_past = cache_fox_k[l][page_table].reshape(dec_bsz, past_len, H_C, DH_C)
        v_past = cache_fox_v[l][page_table].reshape(dec_bsz, past_len, H_C, DH_C)
        f_past = cache_fox_logf[l][page_table].reshape(dec_bsz, past_len, H_C)
        xs, sa, sb, k_new, v_new, f_new = hybrid_layer(
            xs, c_sample, weights, state_gla[l], state_hgrn[l], k_past, v_past, f_past)
        gla_s.append(sa); hg_s.append(sb); ksm.append(k_new); vsm.append(v_new); fsm.append(f_new)
    return (xp, xs,
            jnp.stack(gla_p), jnp.stack(gla_s), jnp.stack(hg_p), jnp.stack(hg_s),
            jnp.stack(kp), jnp.stack(vp), jnp.stack(fp),
            jnp.stack(ksm), jnp.stack(vsm), jnp.stack(fsm))
```

```python
import functools

import jax
import jax.numpy as jnp
from jax import lax
from jax.experimental import pallas as pl
from jax.experimental.pallas import tpu as pltpu

F32 = jnp.float32
BF16 = jnp.bfloat16

N_HEADS = 4
DK_GLA = 64
HEAD_W = 128
GLA_RANK = 16
GLA_GATE_NORM = 16.0
N_BRANCH = 3
LN_EPS = 1e-5
RMS_EPS = 1e-6
F_FLOOR = 1e-30
MASK_VALUE = -1e30

LANES = 128
SUBLANES = 8
VMEM_BYTES = 64 * 1024 * 1024

ROWS_INPROJ = 512
SCAN_CHUNK = 64
ATTN_BLOCK = 256
DECODE_ROWS = 8
SCAN_SAFE_EXP = 80.0

_BW = N_HEADS * HEAD_W
_GROUPS = ("qka", "va", "ga", "qb", "fb", "ib", "gb", "qc", "kc", "vc", "gc", "small")
_WIDTHS = dict(qka=_BW, va=_BW, ga=_BW, qb=_BW, fb=_BW, ib=_BW, gb=_BW, qc=_BW, kc=_BW, vc=_BW,
               gc=_BW, small=LANES)
_OFFSETS = {}
_acc = 0
for _g in _GROUPS:
    _OFFSETS[_g] = _acc
    _acc += _WIDTHS[_g]
N_PROJ = _acc


def _dot(a, b):
    return jnp.dot(a, b, preferred_element_type=F32)


def _dot_nt(a, b):
    return lax.dot_general(a, b, (((1,), (1,)), ((), ())), preferred_element_type=F32)


def _dot_tn(a, b):
    return lax.dot_general(a, b, (((0,), (0,)), ((), ())), preferred_element_type=F32)


def _split3(x):
    x1 = x.astype(BF16)
    r1 = x - x1.astype(F32)
    x2 = r1.astype(BF16)
    x3 = (r1 - x2.astype(F32)).astype(BF16)
    return x1, x2, x3


def _dot_sel_lhs(sel, x):
    x1, x2, x3 = _split3(x)
    return (_dot(sel, x1) + _dot(sel, x2)) + _dot(sel, x3)


def _dot_sel_rhs(x, sel):
    x1, x2, x3 = _split3(x)
    return (_dot(x1, sel) + _dot(x2, sel)) + _dot(x3, sel)


def _log_sigmoid(x):
    return jnp.minimum(x, 0.0) - jnp.log1p(jnp.exp(-jnp.abs(x)))


def _sigmoid(x):
    return 1.0 / (1.0 + jnp.exp(-x))


def _silu(x):
    return x * _sigmoid(x)


def _lower_tri(n):
    r = lax.broadcasted_iota(jnp.int32, (n, n), 0)
    c = lax.broadcasted_iota(jnp.int32, (n, n), 1)
    return jnp.where(r >= c, 1.0, 0.0).astype(BF16)


def _params(semantics, vmem_mb):
    return pltpu.CompilerParams(dimension_semantics=semantics, vmem_limit_bytes=vmem_mb * 1024 * 1024)


def _resident(block_shape, index_map):
    return pl.BlockSpec(block_shape, index_map, pipeline_mode=pl.Buffered(1))


def _row_tiling(t, d, rows_per_seq):
    if rows_per_seq == 1:
        tm = min(ROWS_INPROJ, t)
        return tm, pl.BlockSpec((tm, d), lambda i: (i, 0))
    tm = min(ROWS_INPROJ, rows_per_seq)
    tiles_per_seq = rows_per_seq // tm
    return tm, pl.BlockSpec((None, 1, d), lambda i: (i // tiles_per_seq, 0, 0))


def _lower_bound_kernel(x_ref, o_ref):
    x = x_ref[...]
    depth = x.shape[0]
    e = jnp.exp(x - jnp.max(x, axis=0, keepdims=True))
    p = e / jnp.sum(e, axis=0, keepdims=True)
    acc = jnp.zeros_like(p[0:1])
    for l in range(depth):
        acc = acc + p[l:l + 1]
        o_ref[l:l + 1, :] = jnp.clip(acc - p[0:1], 0.0, 1.0)


def _lower_bounds(logits):
    return pl.pallas_call(
        _lower_bound_kernel, out_shape=jax.ShapeDtypeStruct(logits.shape, F32), name="hgrn_lower_bounds",
    )(logits.astype(F32))


def _mod_kernel(c_ref, w_ref, b_ref, o_ref):
    c = c_ref[...]
    o_ref[...] = _dot(_silu(c).astype(BF16), w_ref[...].astype(BF16)) + b_ref[...]


def _modulation(c_all, w_ada, b_ada):
    depth, d, n3 = w_ada.shape
    rows = c_all.shape[0]
    tn = 512
    return pl.pallas_call(
        _mod_kernel,
        grid=(depth, n3 // tn),
        in_specs=[pl.BlockSpec((rows, d), lambda l, j: (0, 0)),
                  pl.BlockSpec((None, d, tn), lambda l, j: (l, 0, j)),
                  pl.BlockSpec((None, 1, tn), lambda l, j: (l, 0, j))],
        out_specs=pl.BlockSpec((None, rows, tn), lambda l, j: (l, 0, j)),
        out_shape=jax.ShapeDtypeStruct((depth, rows, n3), F32),
        compiler_params=_params(("arbitrary", "arbitrary"), 32),
        name="adaln_modulation",
    )(c_all, w_ada, b_ada.reshape(depth, 1, n3))


def _inproj_kernel(x_ref, shift_ref, scale_ref, w_ref, wa2_ref, ba2_ref, lb_ref, bf_ref,
                   qa_ref, ka_ref, va_ref, sga_ref, ga_ref,
                   sqb_ref, lf_ref, kb_ref, ib_ref, sgb_ref,
                   qc_ref, kc_ref, vc_ref, sgc_ref, zs_ref, logf_ref):
    u = (x_ref[...] * (1.0 + scale_ref[...]) + shift_ref[...]).astype(BF16)

    def proj(name):
        off = _OFFSETS[name]
        return _dot(u, w_ref[:, off:off + _WIDTHS[name]])

    zqk = proj("qka")
    half = N_HEADS * DK_GLA
    qa_ref[...] = (zqk[:, :half] * (DK_GLA ** -0.5)).astype(qa_ref.dtype)
    ka_ref[...] = zqk[:, half:].astype(ka_ref.dtype)
    va_ref[...] = proj("va").astype(va_ref.dtype)
    sga_ref[...] = _silu(proj("ga")).astype(sga_ref.dtype)
    zs = proj("small")
    pre = _dot(zs.astype(BF16), wa2_ref[...]) + ba2_ref[...]
    ga_ref[...] = _log_sigmoid(pre) * (1.0 / GLA_GATE_NORM)
    lane = lax.broadcasted_iota(jnp.int32, zs.shape, 1)
    logf = jnp.where(lane < N_HEADS, _log_sigmoid(zs + bf_ref[...]), 0.0)
    zs_ref[...] = logf
    logf_ref[...] = logf[:, :N_HEADS]
    sqb_ref[...] = _silu(proj("qb")).astype(sqb_ref.dtype)
    fb = proj("fb")
    lb = lb_ref[...]
    f = lb + (1.0 - lb) * _sigmoid(fb)
    lf_ref[...] = jnp.log(jnp.maximum(f, F_FLOOR))
    kb_ref[...] = ((1.0 - lb) * _sigmoid(-fb)).astype(kb_ref.dtype)
    ib_ref[...] = proj("ib").astype(ib_ref.dtype)
    sgb_ref[...] = _silu(proj("gb")).astype(sgb_ref.dtype)
    qc_ref[...] = (proj("qc") * (HEAD_W ** -0.5)).astype(qc_ref.dtype)
    kc_ref[...] = proj("kc")
    vc_ref[...] = proj("vc")
    sgc_ref[...] = _silu(proj("gc")).astype(sgc_ref.dtype)


_INPROJ_OUTS = (
    ("qa", N_HEADS * DK_GLA, BF16), ("ka", N_HEADS * DK_GLA, BF16), ("va", _BW, BF16), ("sga", _BW, BF16),
    ("ga", N_HEADS * DK_GLA, F32),
    ("sqb", _BW, BF16), ("lf", _BW, F32), ("kb", _BW, BF16), ("ib", _BW, BF16), ("sgb", _BW, BF16),
    ("qc", _BW, BF16), ("kc", _BW, F32), ("vc", _BW, F32), ("sgc", _BW, BF16),
    ("zs", LANES, F32), ("logf", N_HEADS, F32))


def _inproj(x, shift, scale, w_l, wa2_l, ba2_l, lb_l, bf_l, rows_per_seq):
    t, d = x.shape
    tm, mod_spec = _row_tiling(t, d, rows_per_seq)
    in_specs = [pl.BlockSpec((tm, d), lambda i: (i, 0)), mod_spec, mod_spec,
                _resident((d, N_PROJ), lambda i: (0, 0)),
                _resident(wa2_l.shape, lambda i: (0, 0)),
                _resident(ba2_l.shape, lambda i: (0, 0)),
                _resident(lb_l.shape, lambda i: (0, 0)),
                _resident(bf_l.shape, lambda i: (0, 0))]
    out_specs = [pl.BlockSpec((tm, w), lambda i: (i, 0)) for _, w, _ in _INPROJ_OUTS]
    out_shape = [jax.ShapeDtypeStruct((t, w), dt) for _, w, dt in _INPROJ_OUTS]
    outs = pl.pallas_call(
        _inproj_kernel, grid=(t // tm,), in_specs=in_specs, out_specs=out_specs, out_shape=out_shape,
        compiler_params=_params(("arbitrary",), 56), name="in_projection",
    )(x, shift, scale, w_l, wa2_l, ba2_l, lb_l, bf_l)
    return {name: o for (name, _, _), o in zip(_INPROJ_OUTS, outs)}


def _scan_kernel(q_ref, k_ref, v_ref, g_ref, o_ref, s_ref, st_scr, kf_scr, bf_scr, *, sub):
    c_idx = pl.program_id(1)
    n_chunks = pl.num_programs(1)
    chunk = q_ref.shape[0]
    n_units = q_ref.shape[1] // LANES
    dk = LANES // sub

    @pl.when(c_idx == 0)
    def _():
        st_scr[...] = jnp.zeros_like(st_scr)

    tri = _lower_tri(chunk)
    row_cc = lax.broadcasted_iota(jnp.int32, (chunk, chunk), 0)
    col_cc = lax.broadcasted_iota(jnp.int32, (chunk, chunk), 1)
    causal = row_cc >= col_cc
    lane = lax.broadcasted_iota(jnp.int32, (chunk, LANES), 1)

    for u in range(n_units):
        sl = slice(u * LANES, (u + 1) * LANES)
        g = g_ref[:, sl]
        b = _dot_sel_lhs(tri, g)
        b_mid = b[chunk // 2 - 1:chunk // 2, :]
        b_end = b[chunk - 1:chunk, :]
        a = b - b_mid
        q = q_ref[:, sl].astype(F32)
        k = k_ref[:, sl].astype(F32)
        q_in = (q * jnp.exp(b)).astype(BF16)
        k_out = k * jnp.exp(b_end - b)
        decay_end = jnp.exp(b_end)
        fast = jnp.max(jnp.abs(a)) <= SCAN_SAFE_EXP

        for j in range(sub):
            h = u * sub + j
            hs = slice(h * HEAD_W, (h + 1) * HEAD_W)
            if sub == 1:
                k_h, k_out_h = k, k_out
            else:
                in_head = (lane // dk) == j
                k_h = jnp.where(in_head, k, 0.0)
                k_out_h = jnp.where(in_head, k_out, 0.0)

            def factorised(q=q, k_h=k_h, a=a):
                qs = (q * jnp.exp(a)).astype(BF16)
                ks = (k_h * jnp.exp(-a)).astype(BF16)
                return jnp.where(causal, _dot_nt(qs, ks), 0.0)

            def direct(q=q, k_h=k_h, b=b):
                kf_scr[...] = k_h
                bf_scr[...] = b

                def body(s, acc):
                    k_row = kf_scr[pl.ds(s, 1), :]
                    b_row = bf_scr[pl.ds(s, 1), :]
                    w = q * k_row * jnp.exp(jnp.minimum(b - b_row, 0.0))
                    col = jnp.sum(w, axis=1, keepdims=True)
                    return jnp.where(col_cc == s, col, acc)

                acc = lax.fori_loop(0, chunk, body, jnp.zeros((chunk, chunk), F32))
                return jnp.where(causal, acc, 0.0)

            scores = lax.cond(fast, factorised, direct)
            v = v_ref[:, hs]
            st = st_scr[h]
            o = _dot(scores.astype(BF16), v) + _dot_nt(q_in, st.astype(BF16))
            o_ref[:, hs] = o.astype(o_ref.dtype)
            st_scr[h] = st * decay_end + _dot_tn(v, k_out_h.astype(BF16))

    @pl.when(c_idx == n_chunks - 1)
    def _():
        for h in range(N_HEADS):
            j = h % sub
            s_full = jnp.transpose(st_scr[h])
            s_ref[h] = s_full[j * dk:(j + 1) * dk, :]


def _scan_prompt(q, k, v, g, batch, seq, dk):
    t = q.shape[0]
    sub = LANES // dk
    chunk = min(SCAN_CHUNK, seq)
    n_chunks = seq // chunk
    wq = q.shape[1]

    def rows(b, c):
        return (b * n_chunks + c, 0)

    o, s = pl.pallas_call(
        functools.partial(_scan_kernel, sub=sub),
        grid=(batch, n_chunks),
        in_specs=[pl.BlockSpec((chunk, wq), rows), pl.BlockSpec((chunk, wq), rows),
                  pl.BlockSpec((chunk, _BW), rows), pl.BlockSpec((chunk, wq), rows)],
        out_specs=[pl.BlockSpec((chunk, _BW), rows),
                   pl.BlockSpec((None, N_HEADS, dk, HEAD_W), lambda b, c: (b, 0, 0, 0))],
        out_shape=[jax.ShapeDtypeStruct((t, _BW), BF16),
                   jax.ShapeDtypeStruct((batch, N_HEADS, dk, HEAD_W), F32)],
        scratch_shapes=[pltpu.VMEM((N_HEADS, HEAD_W, LANES), F32),
                        pltpu.VMEM((chunk, LANES), F32), pltpu.VMEM((chunk, LANES), F32)],
        compiler_params=_params(("arbitrary", "arbitrary"), 32),
        name="linear_scan_prompt",
    )(q, k, v, g)
    return o, s


def _scan_step_kernel(q_ref, k_ref, v_ref, g_ref, s_ref, o_ref, so_ref):
    rows = q_ref.shape[0]
    dk = s_ref.shape[2]
    for i in range(rows):
        for h in range(N_HEADS):
            ks = slice(h * dk, (h + 1) * dk)
            vs = slice(h * HEAD_W, (h + 1) * HEAD_W)
            decay = jnp.transpose(jnp.exp(g_ref[i:i + 1, ks]))
            k_col = jnp.transpose(k_ref[i:i + 1, ks].astype(F32))
            q_col = jnp.transpose(q_ref[i:i + 1, ks].astype(F32))
            v_row = v_ref[i:i + 1, vs].astype(F32)
            s_new = decay * s_ref[i, h] + k_col * v_row
            so_ref[i, h] = s_new
            o_ref[i:i + 1, vs] = jnp.sum(q_col * s_new, axis=0, keepdims=True).astype(o_ref.dtype)


def _scan_step(q, k, v, g, state):
    n, _, dk, dv = state.shape
    rows = min(DECODE_ROWS, n)
    wq = q.shape[1]
    return pl.pallas_call(
        _scan_step_kernel,
        grid=(n // rows,),
        in_specs=[pl.BlockSpec((rows, wq), lambda i: (i, 0)), pl.BlockSpec((rows, wq), lambda i: (i, 0)),
                  pl.BlockSpec((rows, _BW), lambda i: (i, 0)), pl.BlockSpec((rows, wq), lambda i: (i, 0)),
                  pl.BlockSpec((rows, N_HEADS, dk, dv), lambda i: (i, 0, 0, 0))],
        out_specs=[pl.BlockSpec((rows, _BW), lambda i: (i, 0)),
                   pl.BlockSpec((rows, N_HEADS, dk, dv), lambda i: (i, 0, 0, 0))],
        out_shape=[jax.ShapeDtypeStruct((n, _BW), BF16), jax.ShapeDtypeStruct(state.shape, F32)],
        compiler_params=_params(("arbitrary",), 32),
        name="linear_scan_step",
    )(q, k, v, g, state)


def _forget_cumsum_kernel(z_ref, o_ref, carry_scr):
    @pl.when(pl.program_id(1) == 0)
    def _():
        carry_scr[...] = jnp.zeros_like(carry_scr)

    x = z_ref[...]
    n = x.shape[0]
    cum = _dot_sel_lhs(_lower_tri(n), x) + carry_scr[...]
    carry_scr[...] = cum[n - 1:n, :]
    o_ref[...] = jnp.transpose(cum)[:SUBLANES, :]


def _forget_cumsum(zs, batch, seq):
    blk = min(ATTN_BLOCK, seq)
    nb = seq // blk
    return pl.pallas_call(
        _forget_cumsum_kernel,
        grid=(batch, nb),
        in_specs=[pl.BlockSpec((blk, LANES), lambda b, c: (b * nb + c, 0))],
        out_specs=pl.BlockSpec((None, None, SUBLANES, blk), lambda b, c: (b, c, 0, 0)),
        out_shape=jax.ShapeDtypeStruct((batch, nb, SUBLANES, blk), F32),
        scratch_shapes=[pltpu.VMEM((1, LANES), F32)],
        compiler_params=_params(("arbitrary", "arbitrary"), 32),
        name="forget_cumsum",
    )(zs)


def _fox_attn_kernel(q_ref, k_ref, v_ref, cum_ref, o_ref, k_scr, v_scr):
    h = pl.program_id(1)
    qi = pl.program_id(2)
    blk = q_ref.shape[0]

    @pl.when(qi == 0)
    def _():
        k_scr[...] = k_ref[...].astype(BF16)
        v_scr[...] = v_ref[...].astype(BF16)

    q = q_ref[...]
    cq = jnp.transpose(cum_ref[qi, pl.ds(h, 1), :])

    def block(ki, carry, masked):
        m, l, acc = carry
        rows = pl.ds(pl.multiple_of(ki * blk, blk), blk)
        s = _dot_nt(q, k_scr[rows, :]) + cq - cum_ref[ki, pl.ds(h, 1), :]
        if masked:
            r = lax.broadcasted_iota(jnp.int32, s.shape, 0)
            c = lax.broadcasted_iota(jnp.int32, s.shape, 1)
            s = jnp.where(r >= c, s, MASK_VALUE)
        m_new = jnp.maximum(m, jnp.max(s, axis=1, keepdims=True))
        p = jnp.exp(s - m_new)
        alpha = jnp.exp(m - m_new)
        l = alpha * l + jnp.sum(p, axis=1, keepdims=True)
        acc = alpha * acc + _dot(p.astype(BF16), v_scr[rows, :])
        return m_new, l, acc

    init = (jnp.full((blk, 1), MASK_VALUE, F32), jnp.zeros((blk, 1), F32), jnp.zeros((blk, HEAD_W), F32))
    carry = lax.fori_loop(0, qi, lambda ki, c: block(ki, c, False), init)
    _, l, acc = block(qi, carry, True)
    o_ref[...] = (acc / l).astype(o_ref.dtype)


def _fox_attn_prompt(qc, kc, vc, cum_t, batch, seq):
    t = qc.shape[0]
    blk = min(ATTN_BLOCK, seq)
    nb = seq // blk
    return pl.pallas_call(
        _fox_attn_kernel,
        grid=(batch, N_HEADS, nb),
        in_specs=[pl.BlockSpec((blk, HEAD_W), lambda b, h, i: (b * nb + i, h)),
                  pl.BlockSpec((seq, HEAD_W), lambda b, h, i: (b, h)),
                  pl.BlockSpec((seq, HEAD_W), lambda b, h, i: (b, h)),
                  pl.BlockSpec((None, nb, SUBLANES, blk), lambda b, h, i: (b, 0, 0, 0))],
        out_specs=pl.BlockSpec((blk, HEAD_W), lambda b, h, i: (b * nb + i, h)),
        out_shape=jax.ShapeDtypeStruct((t, _BW), BF16),
        scratch_shapes=[pltpu.VMEM((seq, HEAD_W), BF16), pltpu.VMEM((seq, HEAD_W), BF16)],
        compiler_params=_params(("arbitrary", "arbitrary", "arbitrary"), 32),
        name="fox_attention_prompt",
    )(qc, kc, vc, cum_t)


def _fox_decode_kernel(pt_ref, q_ref, kn_ref, vn_ref, lfn_ref, *refs, n_pages):
    del pt_ref
    k_refs = refs[:n_pages]
    v_refs = refs[n_pages:2 * n_pages]
    lf_refs = refs[2 * n_pages:3 * n_pages]
    o_ref = refs[3 * n_pages]
    page = k_refs[0].shape[0]
    width = q_ref.shape[1]
    rows = n_pages * SUBLANES

    row8 = lax.broadcasted_iota(jnp.int32, (SUBLANES, width), 0)
    lane = lax.broadcasted_iota(jnp.int32, (SUBLANES, width), 1)
    own_block = (lane // HEAD_W) == row8
    q_f = jnp.where(own_block, jnp.broadcast_to(q_ref[...].astype(F32), (SUBLANES, width)), 0.0)
    q_bd = q_f.astype(BF16)

    head_lane = (lane % N_HEADS) == row8
    tiles = [jnp.where(head_lane, jnp.broadcast_to(lf_refs[j][...], (SUBLANES, width)), 0.0)
             for j in range(n_pages)]
    lf_rows = jnp.concatenate(tiles, axis=0)
    r_i = lax.broadcasted_iota(jnp.int32, (width, page), 0)
    c_s = lax.broadcasted_iota(jnp.int32, (width, page), 1)
    within = jnp.where((r_i // N_HEADS) <= c_s, 1.0, 0.0).astype(BF16)
    cum_page = _dot_sel_rhs(lf_rows, within)
    totals = jnp.broadcast_to(cum_page[:, page - 1:page], (rows, page))
    r_r = lax.broadcasted_iota(jnp.int32, (rows, rows), 0)
    c_r = lax.broadcasted_iota(jnp.int32, (rows, rows), 1)
    earlier = jnp.where(((r_r % SUBLANES) == (c_r % SUBLANES)) & ((c_r // SUBLANES) < (r_r // SUBLANES)),
                        1.0, 0.0).astype(BF16)
    ck = cum_page + _dot_sel_lhs(earlier, totals)
    lfn_col = jnp.transpose(lfn_ref[...])[:SUBLANES, :]
    cq = ck[rows - SUBLANES:, page - 1:page] + lfn_col

    s_new =jnp.sum(q_f * kn_ref[...], axis=1, keepdims=True)
    logits = []
    m = s_new
    for j in range(n_pages):
        s = _dot_nt(q_bd, k_refs[j][...].astype(BF16)) + cq - ck[j * SUBLANES:(j + 1) * SUBLANES, :]
        logits.append(s)
        m = jnp.maximum(m, jnp.max(s, axis=1, keepdims=True))
    p_new = jnp.exp(s_new - m)
    l = p_new
    acc = p_new * vn_ref[...]
    for j in range(n_pages):
        p = jnp.exp(logits[j] - m)
        l = l + jnp.sum(p, axis=1, keepdims=True)
        acc = acc + _dot(p.astype(BF16), v_refs[j][...].astype(BF16))
    out = jnp.where(own_block, acc / l, 0.0)
    o_ref[...] = jnp.sum(out, axis=0, keepdims=True).astype(o_ref.dtype)


def _fox_decode(layer, page_table, qc, kn, vn, lfn, cache_k, cache_v, cache_lf):
    n, n_pages = page_table.shape
    width = qc.shape[1]
    page = cache_k.shape[2]

    def vec(a):
        return a.reshape(n, 1, a.shape[1])

    vec_spec = pl.BlockSpec((None, 1, width), lambda b, pt: (b, 0, 0))

    def page_specs(block):
        return [pl.BlockSpec((None, None) + block, functools.partial(
            lambda b, pt, j: (layer, pt[b, j], 0, 0), j=j)) for j in range(n_pages)]

    grid_spec = pltpu.PrefetchScalarGridSpec(
        num_scalar_prefetch=1, grid=(n,),
        in_specs=[vec_spec, vec_spec, vec_spec, pl.BlockSpec((None, 1, LANES), lambda b, pt: (b, 0, 0))]
        + page_specs((page, width)) + page_specs((page, width)) + page_specs((1, page * N_HEADS)),
        out_specs=vec_spec)
    out = pl.pallas_call(
        functools.partial(_fox_decode_kernel, n_pages=n_pages),
        grid_spec=grid_spec,
        out_shape=jax.ShapeDtypeStruct((n, 1, width), BF16),
        compiler_params=_params(("arbitrary",), 48),
        name="fox_attention_decode",
    )(page_table, vec(qc), vec(kn), vec(vn), vec(lfn),
      *([cache_k] * n_pages), *([cache_v] * n_pages), *([cache_lf] * n_pages))
    return out.reshape(n, width)


def _merge_kernel(x_ref, shift_ref, scale_ref, gate_ref, oa_ref, ob_ref, oc_ref, sga_ref, sgb_ref, sgc_ref,
                  wmg_ref, wbr_ref, wout_ref, gna_ref, gnb_ref, lng_ref, lnb_ref, o_ref, *, alpha):
    x = x_ref[...]
    d = x.shape[1]
    u = (x * (1.0 + scale_ref[...]) + shift_ref[...]).astype(BF16)

    def head_rmsnorm(o, gain):
        parts = []
        for h in range(N_HEADS):
            blk = o[:, h * HEAD_W:(h + 1) * HEAD_W]
            parts.append(blk * lax.rsqrt(jnp.mean(blk * blk, axis=1, keepdims=True) + RMS_EPS))
        return jnp.concatenate(parts, axis=1) * gain

    branches = (
        head_rmsnorm(oa_ref[...].astype(F32), gna_ref[...]) * sga_ref[...].astype(F32),
        head_rmsnorm(ob_ref[...].astype(F32), gnb_ref[...]) * sgb_ref[...].astype(F32),
        oc_ref[...].astype(F32) * sgc_ref[...].astype(F32))
    merged = jnp.zeros(x.shape, F32)
    for n in range(N_BRANCH):
        gate_n = _sigmoid(_dot(u, wmg_ref[:, n * d:(n + 1) * d]))
        merged = merged + gate_n * _dot(branches[n].astype(BF16), wbr_ref[n])
    y = _dot(merged.astype(BF16), wout_ref[...])
    r = alpha * x + gate_ref[...] * y
    mu = jnp.mean(r, axis=1, keepdims=True)
    rc = r - mu
    var = jnp.mean(rc * rc, axis=1, keepdims=True)
    o_ref[...] = rc * lax.rsqrt(var + LN_EPS) * lng_ref[...] + lnb_ref[...]


def _merge(x, shift, scale, gate, o_a, o_b, o_c, sga, sgb, sgc, wmg_l, wbr_l, wout_l, gna_l, gnb_l,
           lng_l, lnb_l, rows_per_seq, alpha):
    t, d = x.shape
    tm, mod_spec = _row_tiling(t, d, rows_per_seq)
    row_spec = pl.BlockSpec((tm, d), lambda i: (i, 0))
    br_spec = pl.BlockSpec((tm, _BW), lambda i: (i, 0))
    in_specs = [row_spec, mod_spec, mod_spec, mod_spec] + [br_spec] * 6 + [
        _resident(wmg_l.shape, lambda i: (0, 0)), _resident(wbr_l.shape, lambda i: (0, 0, 0)),
        _resident(wout_l.shape, lambda i: (0, 0)),
        _resident(gna_l.shape, lambda i: (0, 0)), _resident(gnb_l.shape, lambda i: (0, 0)),
        _resident(lng_l.shape, lambda i: (0, 0)), _resident(lnb_l.shape, lambda i: (0, 0))]
    return pl.pallas_call(
        functools.partial(_merge_kernel, alpha=alpha),
        grid=(t // tm,), in_specs=in_specs, out_specs=row_spec,
        out_shape=jax.ShapeDtypeStruct((t, d), F32),
        compiler_params=_params(("arbitrary",), 56), name="merge_out_projection",
    )(x, shift, scale, gate, o_a, o_b, o_c, sga, sgb, sgc, wmg_l, wbr_l, wout_l, gna_l, gnb_l, lng_l, lnb_l)


def _permute_in_weights(w_in):
    widths = (N_HEADS * DK_GLA, N_HEADS * DK_GLA, _BW, _BW, GLA_RANK,
              _BW, _BW, _BW, _BW, _BW, _BW, _BW, N_HEADS, _BW)
    names = ("qa", "ka", "va", "ga", "ra", "qb", "fb", "ib", "gb", "qc", "kc", "vc", "fc", "gc")
    cols, off = {}, 0
    for name, w in zip(names, widths):
        cols[name] = w_in[:, :, off:off + w]
        off += w
    mg = w_in[:, :, off:]
    pad = jnp.zeros(w_in.shape[:2] + (LANES - N_HEADS - GLA_RANK,), w_in.dtype)
    order = [cols[n] for n in ("qa", "ka", "va", "ga", "qb", "fb", "ib", "gb", "qc", "kc", "vc", "gc")]
    order += [cols["fc"], cols["ra"], pad]
    return jnp.concatenate(order, axis=2).astype(BF16), mg.astype(BF16)


def kernel(x_prompt, x_sample, cache_fox_k, cache_fox_v, cache_fox_logf, state_gla, state_hgrn, page_table,
           c_prompt, c_sample, w_in, w_gla_a2, b_gla_a2, norm_gla, hgrn_lb_logits, norm_hgrn, b_fox_f,
           w_branch, w_out, w_ada, b_ada, ln_g, ln_b):
    batch, seq, d = x_prompt.shape
    n_dec = x_sample.shape[0]
    depth = w_in.shape[0]
    n_phys, page = cache_fox_k.shape[1], cache_fox_k.shape[2]
    alpha = (2 * depth) ** 0.25

    w_proj, w_mg = _permute_in_weights(w_in)
    wa2 = jnp.zeros((depth, LANES, N_HEADS * DK_GLA), F32).at[:, N_HEADS:N_HEADS + GLA_RANK, :].set(w_gla_a2)
    wa2 = wa2.astype(BF16)
    ba2 = b_gla_a2.reshape(depth, 1, -1)
    bf_pad = jnp.zeros((depth, 1, LANES), F32).at[:, 0, :N_HEADS].set(b_fox_f)
    w_br = w_branch.astype(BF16)
    w_o = w_out.astype(BF16)
    gna = norm_gla.reshape(depth, 1, -1)
    gnb = norm_hgrn.reshape(depth, 1, -1)
    lng = ln_g.reshape(depth, 1, -1)
    lnb = ln_b.reshape(depth, 1, -1)
    cache_k = cache_fox_k.reshape(depth, n_phys, page, N_HEADS * HEAD_W)
    cache_v = cache_fox_v.reshape(depth, n_phys, page, N_HEADS * HEAD_W)
    cache_lf = cache_fox_logf.reshape(depth, n_phys, 1, page * N_HEADS)

    lower = _lower_bounds(hgrn_lb_logits).reshape(depth, 1, -1)
    mod = _modulation(jnp.concatenate([c_prompt, c_sample], axis=0), w_ada, b_ada)

    xp = x_prompt.reshape(batch * seq, d)
    xs = x_sample.reshape(n_dec, d)
    outs = {k: [] for k in ("gla_p", "gla_s", "hg_p", "hg_s", "kp", "vp", "fp", "ks", "vs", "fs")}
    for l in range(depth):
        mod_p = mod[l, :batch].reshape(batch, 1, 3 * d)
        shift_p, scale_p, gate_p = mod_p[:, :, :d], mod_p[:, :, d:2 * d], mod_p[:, :, 2 * d:]
        mod_s = mod[l, batch:]
        shift_s, scale_s, gate_s = mod_s[:, :d], mod_s[:, d:2 * d], mod_s[:, 2 * d:]

        z = _inproj(xp, shift_p, scale_p, w_proj[l], wa2[l], ba2[l], lower[l], bf_pad[l], seq)
        o_a, s_a = _scan_prompt(z["qa"], z["ka"], z["va"], z["ga"], batch, seq, DK_GLA)
        o_b, s_b = _scan_prompt(z["sqb"], z["kb"], z["ib"], z["lf"], batch, seq, HEAD_W)
        cum_t = _forget_cumsum(z["zs"], batch, seq)
        o_c = _fox_attn_prompt(z["qc"], z["kc"], z["vc"], cum_t, batch, seq)
        xp = _merge(xp, shift_p, scale_p, gate_p, o_a, o_b, o_c, z["sga"], z["sgb"], z["sgc"],
                    w_mg[l], w_br[l], w_o[l], gna[l], gnb[l], lng[l], lnb[l], seq, alpha)
        outs["gla_p"].append(s_a)
        outs["hg_p"].append(s_b)
        outs["kp"].append(z["kc"])
        outs["vp"].append(z["vc"])
        outs["fp"].append(z["logf"])

        z = _inproj(xs, shift_s, scale_s, w_proj[l], wa2[l], ba2[l], lower[l], bf_pad[l], 1)
        o_a, s_a = _scan_step(z["qa"], z["ka"], z["va"], z["ga"], state_gla[l])
        o_b, s_b = _scan_step(z["sqb"], z["kb"], z["ib"], z["lf"], state_hgrn[l])
        o_c = _fox_decode(l, page_table, z["qc"], z["kc"], z["vc"], z["zs"], cache_k, cache_v, cache_lf)
        xs = _merge(xs, shift_s, scale_s, gate_s, o_a, o_b, o_c, z["sga"], z["sgb"], z["sgc"],
                    w_mg[l], w_br[l], w_o[l], gna[l], gnb[l], lng[l], lnb[l], 1, alpha)
        outs["gla_s"].append(s_a)
        outs["hg_s"].append(s_b)
        outs["ks"].append(z["kc"])
        outs["vs"].append(z["vc"])
        outs["fs"].append(z["logf"])

    def stack(name, shape):
        return jnp.stack(outs[name]).reshape((depth,) + shape)

    return (xp.reshape(batch, seq, d), xs.reshape(n_dec, 1, d),
            jnp.stack(outs["gla_p"]), jnp.stack(outs["gla_s"]),
            jnp.stack(outs["hg_p"]), jnp.stack(outs["hg_s"]),
            stack("kp", (batch, seq, N_HEADS, HEAD_W)), stack("vp", (batch, seq, N_HEADS, HEAD_W)),
            stack("fp", (batch, seq, N_HEADS)),
            stack("ks", (n_dec, 1, N_HEADS, HEAD_W)), stack("vs", (n_dec, 1, N_HEADS, HEAD_W)),
            stack("fs", (n_dec, 1, N_HEADS)))
```

```python
import functools

import jax
import jax.numpy as jnp
from jax import lax
from jax.experimental import pallas as pl
from jax.experimental.pallas import tpu as pltpu

F32 = jnp.float32
BF16 = jnp.bfloat16

N_HEADS = 4
DK_GLA = 64
HEAD_W = 128
GLA_RANK = 16
GLA_GATE_NORM = 16.0
N_BRANCH = 3
LN_EPS = 1e-5
RMS_EPS = 1e-6
F_FLOOR = 1e-30
MASK_VALUE = -1e30
LOG2E = 1.4426950408889634

LANES = 128
SUBLANES = 8

ROWS_INPROJ = 512
SCAN_CHUNK = 64
SCAN_ROWS = 256
ATTN_BLOCK = 256
DECODE_ROWS = 8
DECODE_PAGE_GROUPS = 4
SCAN_SAFE_EXP = 80.0

_BW = N_HEADS * HEAD_W
_GROUPS = ("qka", "va", "ga", "qb", "fb", "ib", "gb", "qc", "kc", "vc", "gc", "small")
_WIDTHS = dict(qka=_BW, va=_BW, ga=_BW, qb=_BW, fb=_BW, ib=_BW, gb=_BW, qc=_BW, kc=_BW, vc=_BW,
               gc=_BW, small=LANES)
_OFFSETS = {}
_acc = 0
for _g in _GROUPS:
    _OFFSETS[_g] = _acc
    _acc += _WIDTHS[_g]
N_PROJ = _acc


def _dot(a, b):
    return jnp.dot(a, b, preferred_element_type=F32)


def _dot_nt(a, b):
    return lax.dot_general(a, b, (((1,), (1,)), ((), ())), preferred_element_type=F32)


def _dot_tn(a, b):
    return lax.dot_general(a, b, (((0,), (0,)), ((), ())), preferred_element_type=F32)


def _split3(x):
    x1 = x.astype(BF16)
    r1 = x - x1.astype(F32)
    x2 = r1.astype(BF16)
    x3 = (r1 - x2.astype(F32)).astype(BF16)
    return x1, x2, x3


def _dot_sel_lhs(sel, x):
    x1, x2, x3 = _split3(x)
    return (_dot(sel, x1) + _dot(sel, x2)) + _dot(sel, x3)


def _dot_sel_rhs(x, sel):
    x1, x2, x3 = _split3(x)
    return (_dot(x1, sel) + _dot(x2, sel)) + _dot(x3, sel)


def _log_sigmoid(x):
    return jnp.minimum(x, 0.0) - jnp.log1p(jnp.exp(-jnp.abs(x)))


def _sigmoid(x):
    return 1.0 / (1.0 + jnp.exp(-x))


def _silu(x):
    return x * _sigmoid(x)


def _lower_tri(n):
    r = lax.broadcasted_iota(jnp.int32, (n, n), 0)
    c = lax.broadcasted_iota(jnp.int32, (n, n), 1)
    return jnp.where(r >= c, 1.0, 0.0).astype(BF16)


def _params(semantics, vmem_mb):
    return pltpu.CompilerParams(dimension_semantics=semantics, vmem_limit_bytes=vmem_mb * 1024 * 1024)


def _resident(block_shape, index_map):
    return pl.BlockSpec(block_shape, index_map, pipeline_mode=pl.Buffered(1))


def _row_tiling(t, d, rows_per_seq):
    if rows_per_seq == 1:
        tm = min(ROWS_INPROJ, t)
        return tm, pl.BlockSpec((tm, d), lambda i: (i, 0))
    tm = min(ROWS_INPROJ, rows_per_seq)
    tiles_per_seq = rows_per_seq // tm
    return tm, pl.BlockSpec((None, 1, d), lambda i: (i // tiles_per_seq, 0, 0))


def _lower_bound_kernel(x_ref, o_ref):
    x = x_ref[...]
    depth = x.shape[0]
    e = jnp.exp(x - jnp.max(x, axis=0, keepdims=True))
    p = e / jnp.sum(e, axis=0, keepdims=True)
    acc = jnp.zeros_like(p[0:1])
    for l in range(depth):
        acc = acc + p[l:l + 1]
        o_ref[l:l + 1, :] = jnp.clip(acc - p[0:1], 0.0, 1.0)


def _lower_bounds(logits):
    return pl.pallas_call(
        _lower_bound_kernel, out_shape=jax.ShapeDtypeStruct(logits.shape, F32), name="hgrn_lower_bounds",
    )(logits.astype(F32))


def _mod_kernel(c_ref, w_ref, b_ref, o_ref):
    c = c_ref[...]
    o_ref[...] = _dot(_silu(c).astype(BF16), w_ref[...].astype(BF16)) + b_ref[...]


def _modulation(c_all, w_ada, b_ada):
    depth, d, n3 = w_ada.shape
    rows = c_all.shape[0]
    tn = 512
    return pl.pallas_call(
        _mod_kernel,
        grid=(depth, n3 // tn),
        in_specs=[pl.BlockSpec((rows, d), lambda l, j: (0, 0)),
                  pl.BlockSpec((None, d, tn), lambda l, j: (l, 0, j)),
                  pl.BlockSpec((None, 1, tn), lambda l, j: (l, 0, j))],
        out_specs=pl.BlockSpec((None, rows, tn), lambda l, j: (l, 0, j)),
        out_shape=jax.ShapeDtypeStruct((depth, rows, n3), F32),
        compiler_params=_params(("arbitrary", "arbitrary"), 32),
        name="adaln_modulation",
    )(c_all, w_ada, b_ada.reshape(depth, 1, n3))


def _inproj_kernel(x_ref, shift_ref, scale_ref, w_ref, wa2_ref, ba2_ref, lb_ref, bf_ref,
                   qa_ref, ka_ref, va_ref, sga_ref, ga_ref,
                   sqb_ref, lf_ref, kb_ref, ib_ref, sgb_ref,
                   qc_ref, kc_ref, vc_ref, sgc_ref, zs_ref, logf_ref):
    u = (x_ref[...] * (1.0 + scale_ref[...]) + shift_ref[...]).astype(BF16)
    tm = x_ref.shape[0]

    def proj(name):
        off = _OFFSETS[name]
        return _dot(u, w_ref[:, off:off + _WIDTHS[name]])

    zqk = proj("qka")
    half = N_HEADS * DK_GLA
    qa_ref[...] = (zqk[:, :half] * (DK_GLA ** -0.5)).astype(qa_ref.dtype)
    ka_ref[...] = zqk[:, half:].astype(ka_ref.dtype)
    va_ref[...] = proj("va").astype(va_ref.dtype)
    sga_ref[...] = _silu(proj("ga")).astype(sga_ref.dtype)
    zs = proj("small")
    pre = _dot(zs.astype(BF16), wa2_ref[...]) + ba2_ref[...]
    ga_ref[...] = _log_sigmoid(pre) * (1.0 / GLA_GATE_NORM)
    lane = lax.broadcasted_iota(jnp.int32, zs.shape, 1)
    logf = jnp.where(lane < N_HEADS, _log_sigmoid(zs + bf_ref[...]), 0.0)
    zs_ref[...] = logf
    logf_ref[...] = logf[:, :N_HEADS]
    sqb_ref[...] = _silu(proj("qb")).astype(sqb_ref.dtype)
    fb = proj("fb")
    lb = lb_ref[...]
    f = lb + (1.0 - lb) * _sigmoid(fb)
    lf_ref[...] = jnp.log(jnp.maximum(f, F_FLOOR))
    kb_ref[...] = ((1.0 - lb) * _sigmoid(-fb)).astype(kb_ref.dtype)
    ib_ref[...] = proj("ib").astype(ib_ref.dtype)
    sgb_ref[...] = _silu(proj("gb")).astype(sgb_ref.dtype)
    qc_ref[...] = (proj("qc") * (HEAD_W ** -0.5 * LOG2E)).astype(qc_ref.dtype)
    for name, ref in (("kc", kc_ref), ("vc", vc_ref)):
        z = proj(name)
        for h in range(N_HEADS):
            ref[pl.ds(h, tm, stride=N_HEADS), :] = z[:, h * HEAD_W:(h + 1) * HEAD_W]
    sgc_ref[...] = _silu(proj("gc")).astype(sgc_ref.dtype)


_INPROJ_OUTS = (
    ("qa", 1, N_HEADS * DK_GLA, BF16), ("ka", 1, N_HEADS * DK_GLA, BF16), ("va", 1, _BW, BF16),
    ("sga", 1, _BW, BF16), ("ga", 1, N_HEADS * DK_GLA, F32),
    ("sqb", 1, _BW, BF16), ("lf", 1, _BW, F32), ("kb", 1, _BW, BF16), ("ib", 1, _BW, BF16),
    ("sgb", 1, _BW, BF16),
    ("qc", 1, _BW, BF16), ("kc", N_HEADS, HEAD_W, F32), ("vc", N_HEADS, HEAD_W, F32), ("sgc", 1, _BW, BF16),
    ("zs", 1, LANES, F32), ("logf", 1, N_HEADS, F32))


def _inproj(x, shift, scale, w_all, wa2_l, ba2_l, lb_l, bf_l, layer, rows_per_seq):
    t, d = x.shape
    tm, mod_spec = _row_tiling(t, d, rows_per_seq)
    in_specs = [pl.BlockSpec((tm, d), lambda i: (i, 0)), mod_spec, mod_spec,
                _resident((None, d, N_PROJ), lambda i: (layer, 0, 0)),
                _resident(wa2_l.shape, lambda i: (0, 0)),
                _resident(ba2_l.shape, lambda i: (0, 0)),
                _resident(lb_l.shape, lambda i: (0, 0)),
                _resident(bf_l.shape, lambda i: (0, 0))]
    out_specs = [pl.BlockSpec((tm * r, w), lambda i: (i, 0)) for _, r, w, _ in _INPROJ_OUTS]
    out_shape = [jax.ShapeDtypeStruct((t * r, w), dt) for _, r, w, dt in _INPROJ_OUTS]
    outs = pl.pallas_call(
        _inproj_kernel, grid=(t // tm,), in_specs=in_specs, out_specs=out_specs, out_shape=out_shape,
        compiler_params=_params(("arbitrary",), 56), name="in_projection",
    )(x, shift, scale, w_all, wa2_l, ba2_l, lb_l, bf_l)
    return {name: o for (name, _, _, _), o in zip(_INPROJ_OUTS, outs)}


def _scan_kernel(q_ref, k_ref, v_ref, g_ref, o_ref, s_ref, st_scr, kf_scr, bf_scr, *, sub, chunk):
    step = pl.program_id(1)
    n_steps = pl.num_programs(1)
    n_chunks = q_ref.shape[0] // chunk
    n_units = q_ref.shape[1] // LANES
    dk = LANES // sub

    @pl.when(step == 0)
    def _():
        st_scr[...] = jnp.zeros_like(st_scr)

    tri = _lower_tri(chunk)
    row_cc = lax.broadcasted_iota(jnp.int32, (chunk, chunk), 0)
    col_cc = lax.broadcasted_iota(jnp.int32, (chunk, chunk), 1)
    causal = row_cc >= col_cc
    lane = lax.broadcasted_iota(jnp.int32, (chunk, LANES), 1)
    mid = chunk // 2

    cum, worst = [], jnp.zeros((1, 1), F32)
    for c in range(n_chunks):
        b = _dot_sel_lhs(tri, g_ref[c * chunk:(c + 1) * chunk, :])
        cum.append(b)
        worst = jnp.maximum(worst, jnp.max(jnp.abs(b - b[mid - 1:mid, :]), keepdims=True))
    fast = worst[0, 0] <= SCAN_SAFE_EXP

    def direct_scores(q, k_h, b):
        kf_scr[...] = k_h
        bf_scr[...] = b

        def body(s, acc):
            k_row = kf_scr[pl.ds(s, 1), :]
            b_row = bf_scr[pl.ds(s, 1), :]
            w = q * k_row * jnp.exp(jnp.minimum(b - b_row, 0.0))
            return jnp.where(col_cc == s, jnp.sum(w, axis=1, keepdims=True), acc)

        return lax.fori_loop(0, chunk, body, jnp.zeros((chunk, chunk), F32))

    def run(factorised):
        st = [st_scr[h] for h in range(N_HEADS)]
        for c in range(n_chunks):
            rs = slice(c * chunk, (c + 1) * chunk)
            for u in range(n_units):
                sl = slice(u * LANES, (u + 1) * LANES)
                b = cum[c][:, sl]
                b_end = b[chunk - 1:chunk, :]
                a = b - b[mid - 1:mid, :]
                q = q_ref[rs, sl].astype(F32)
                k = k_ref[rs, sl].astype(F32)
                q_in = (q * jnp.exp(b)).astype(BF16)
                k_out = k * jnp.exp(b_end - b)
                decay_end = jnp.exp(b_end)
                if factorised:
                    qs = (q * jnp.exp(a)).astype(BF16)
                    ks = k * jnp.exp(-a)
                for j in range(sub):
                    h = u * sub + j
                    hs = slice(h * HEAD_W, (h + 1) * HEAD_W)
                    in_head = None if sub == 1 else (lane // dk) == j

                    def own(x, in_head=in_head):
                        return x if in_head is None else jnp.where(in_head, x, 0.0)

                    if factorised:
                        scores = _dot_nt(qs, own(ks).astype(BF16))
                    else:
                        scores = direct_scores(q, own(k), b)
                    scores = jnp.where(causal, scores, 0.0)
                    v = v_ref[rs, hs]
                    o = _dot(scores.astype(BF16), v) + _dot_nt(q_in, st[h].astype(BF16))
                    o_ref[rs, hs] = o.astype(o_ref.dtype)
                    st[h] = st[h] * decay_end + _dot_tn(v, own(k_out).astype(BF16))
        for h in range(N_HEADS):
            st_scr[h] = st[h]

    @pl.when(fast)
    def _():
        run(True)

    @pl.when(jnp.logical_not(fast))
    def _():
        run(False)

    @pl.when(step == n_steps - 1)
    def _():
        for h in range(N_HEADS):
            j = h % sub
            s_full = jnp.transpose(st_scr[h])
            s_ref[h] = s_full[j * dk:(j + 1) * dk, :]


def _scan_prompt(q, k, v, g, batch, seq, dk):
    t = q.shape[0]
    sub = LANES // dk
    rows = min(SCAN_ROWS, seq)
    chunk = min(SCAN_CHUNK, rows)
    n_steps = seq // rows
    wq = q.shape[1]

    def at(b, c):
        return (b * n_steps + c, 0)

    o, s = pl.pallas_call(
        functools.partial(_scan_kernel, sub=sub, chunk=chunk),
        grid=(batch, n_steps),
        in_specs=[pl.BlockSpec((rows, wq), at), pl.BlockSpec((rows, wq), at),
                  pl.BlockSpec((rows, _BW), at), pl.BlockSpec((rows, wq), at)],
        out_specs=[pl.BlockSpec((rows, _BW), at),
                   pl.BlockSpec((None, N_HEADS, dk, HEAD_W), lambda b, c: (b, 0, 0, 0))],
        out_shape=[jax.ShapeDtypeStruct((t, _BW), BF16),
                   jax.ShapeDtypeStruct((batch, N_HEADS, dk, HEAD_W), F32)],
        scratch_shapes=[pltpu.VMEM((N_HEADS, HEAD_W, LANES), F32),
                        pltpu.VMEM((chunk, LANES), F32), pltpu.VMEM((chunk, LANES), F32)],
        compiler_params=_params(("arbitrary", "arbitrary"), 32),
        name="linear_scan_prompt",
    )(q, k, v, g)
    return o, s


def _scan_step_kernel(q_ref, k_ref, v_ref, g_ref, s_ref, o_ref, so_ref):
    rows = q_ref.shape[0]
    dk = s_ref.shape[2]
    for i in range(rows):
        for h in range(N_HEADS):
            ks = slice(h * dk, (h + 1) * dk)
            vs = slice(h * HEAD_W, (h + 1) * HEAD_W)
            decay = jnp.transpose(jnp.exp(g_ref[i:i + 1, ks]))
            k_col = jnp.transpose(k_ref[i:i + 1, ks].astype(F32))
            q_col = jnp.transpose(q_ref[i:i + 1, ks].astype(F32))
            v_row = v_ref[i:i + 1, vs].astype(F32)
            s_new = decay * s_ref[i, h] + k_col * v_row
            so_ref[i, h] = s_new
            o_ref[i:i + 1, vs] = jnp.sum(q_col * s_new, axis=0, keepdims=True).astype(o_ref.dtype)


def _scan_step(q, k, v, g, state_all, layer):
    _, n, _, dk, dv = state_all.shape
    rows = min(DECODE_ROWS, n)
    wq = q.shape[1]
    return pl.pallas_call(
        _scan_step_kernel,
        grid=(n // rows,),
        in_specs=[pl.BlockSpec((rows, wq), lambda i: (i, 0)), pl.BlockSpec((rows, wq), lambda i: (i, 0)),
                  pl.BlockSpec((rows, _BW), lambda i: (i, 0)), pl.BlockSpec((rows, wq), lambda i: (i, 0)),
                  pl.BlockSpec((None, rows, N_HEADS, dk, dv), lambda i: (layer, i, 0, 0, 0))],
        out_specs=[pl.BlockSpec((rows, _BW), lambda i: (i, 0)),
                   pl.BlockSpec((rows, N_HEADS, dk, dv), lambda i: (i, 0, 0, 0))],
        out_shape=[jax.ShapeDtypeStruct((n, _BW), BF16), jax.ShapeDtypeStruct(state_all.shape[1:], F32)],
        compiler_params=_params(("arbitrary",), 32),
        name="linear_scan_step",
    )(q, k, v, g, state_all)


def _forget_cumsum_kernel(z_ref, o_ref, carry_scr):
    @pl.when(pl.program_id(1) == 0)
    def _():
        carry_scr[...] = jnp.zeros_like(carry_scr)

    x = z_ref[...]
    n = x.shape[0]
    cum = _dot_sel_lhs(_lower_tri(n), x) + carry_scr[...]
    carry_scr[...] = cum[n - 1:n, :]
    o_ref[...] = jnp.transpose(cum * LOG2E)[:SUBLANES, :]


def _forget_cumsum(zs, batch, seq):
    blk = min(ATTN_BLOCK, seq)
    nb = seq // blk
    return pl.pallas_call(
        _forget_cumsum_kernel,
        grid=(batch, nb),
        in_specs=[pl.BlockSpec((blk, LANES), lambda b, c: (b * nb + c, 0))],
        out_specs=pl.BlockSpec((None, None, SUBLANES, blk), lambda b, c: (b, c, 0, 0)),
        out_shape=jax.ShapeDtypeStruct((batch, nb, SUBLANES, blk), F32),
        scratch_shapes=[pltpu.VMEM((1, LANES), F32)],
        compiler_params=_params(("arbitrary", "arbitrary"), 32),
        name="forget_cumsum",
    )(zs)


def _fox_attn_kernel(q_ref, k_ref, v_ref, cum_ref, o_ref, k_scr, v_scr):
    qi = pl.program_id(1)
    blk = q_ref.shape[0]
    seq = k_scr.shape[1]

    @pl.when(qi == 0)
    def _():
        for h in range(N_HEADS):
            k_scr[h] = k_ref[pl.ds(h, seq, stride=N_HEADS), :].astype(BF16)
            v_scr[h] = v_ref[pl.ds(h, seq, stride=N_HEADS), :].astype(BF16)

    q = [q_ref[:, h * HEAD_W:(h + 1) * HEAD_W] for h in range(N_HEADS)]

    def block(ki, carry, masked):
        rows = pl.ds(pl.multiple_of(ki * blk, blk), blk)
        out = []
        for h in range(N_HEADS):
            m, l, acc = carry[h]
            s = _dot_nt(q[h], k_scr[h, rows, :]) - cum_ref[ki, h:h + 1, :]
            if masked:
                r = lax.broadcasted_iota(jnp.int32, s.shape, 0)
                c = lax.broadcasted_iota(jnp.int32, s.shape, 1)
                s = jnp.where(r >= c, s, MASK_VALUE)
            m_new = jnp.maximum(m, jnp.max(s, axis=1, keepdims=True))
            p = jnp.exp2(s - m_new)
            alpha = jnp.exp2(m - m_new)
            l = alpha * l + jnp.sum(p, axis=1, keepdims=True)
            acc = alpha * acc + _dot(p.astype(BF16), v_scr[h, rows, :])
            out.append((m_new, l, acc))
        return tuple(out)

    init = tuple((jnp.full((blk, 1), MASK_VALUE, F32), jnp.zeros((blk, 1), F32), jnp.zeros((blk, HEAD_W), F32))
                 for _ in range(N_HEADS))
    carry = lax.fori_loop(0, qi, lambda ki, c: block(ki, c, False), init)
    final = block(qi, carry, True)
    for h in range(N_HEADS):
        _, l, acc = final[h]
        o_ref[:, h * HEAD_W:(h + 1) * HEAD_W] = (acc / l).astype(o_ref.dtype)


def _fox_attn_prompt(qc, kc, vc, cum_t, batch, seq):
    t = qc.shape[0]
    blk = min(ATTN_BLOCK, seq)
    nb = seq // blk
    kv_spec = pl.BlockSpec((seq * N_HEADS, HEAD_W), lambda b, i: (b, 0))
    return pl.pallas_call(
        _fox_attn_kernel,
        grid=(batch, nb),
        in_specs=[pl.BlockSpec((blk, _BW), lambda b, i: (b * nb + i, 0)), kv_spec, kv_spec,
                  pl.BlockSpec((None, nb, SUBLANES, blk), lambda b, i: (b, 0, 0, 0))],
        out_specs=pl.BlockSpec((blk, _BW), lambda b, i: (b * nb + i, 0)),
        out_shape=jax.ShapeDtypeStruct((t, _BW), BF16),
        scratch_shapes=[pltpu.VMEM((N_HEADS, seq, HEAD_W), BF16), pltpu.VMEM((N_HEADS, seq, HEAD_W), BF16)],
        compiler_params=_params(("arbitrary", "arbitrary"), 40),
        name="fox_attention_prompt",
    )(qc, kc, vc, cum_t)


def _fox_decode_kernel(pt_ref, q_ref, kn_ref, vn_ref, lfn_ref, *refs, n_pages):
    del pt_ref
    k_refs = refs[:n_pages]
    v_refs = refs[n_pages:2 * n_pages]
    lf_refs = refs[2 * n_pages:3 * n_pages]
    o_ref = refs[3 * n_pages]
    page = lf_refs[0].shape[1]
    wide = page * N_HEADS
    rows = n_pages * SUBLANES
    pad = jnp.zeros((SUBLANES - N_HEADS, HEAD_W), F32)

    def head_rows(x):
        return jnp.concatenate([x[:, h * HEAD_W:(h + 1) * HEAD_W] for h in range(N_HEADS)] + [pad], axis=0)

    q4 = head_rows(q_ref[...].astype(F32))
    q4b = q4.astype(BF16)
    kn4 = jnp.concatenate([kn_ref[...], pad], axis=0)
    vn4 = jnp.concatenate([vn_ref[...], pad], axis=0)

    lf_rows = jnp.concatenate([x for j in range(n_pages) for x in (lf_refs[j][...], pad)], axis=0)
    r_p = lax.broadcasted_iota(jnp.int32, (page, page), 0)
    c_p = lax.broadcasted_iota(jnp.int32, (page, page), 1)
    upto = jnp.where(r_p <= c_p, 1.0, 0.0).astype(BF16)
    cum_page = _dot_sel_rhs(lf_rows, upto)
    totals = jnp.broadcast_to(cum_page[:, page - 1:page], (rows, page))
    r_r = lax.broadcasted_iota(jnp.int32, (rows, rows), 0)
    c_r = lax.broadcasted_iota(jnp.int32, (rows, rows), 1)
    earlier = jnp.where(((r_r % SUBLANES) == (c_r % SUBLANES)) & ((c_r // SUBLANES) < (r_r // SUBLANES)),
                        1.0, 0.0).astype(BF16)
    ck = cum_page + _dot_sel_lhs(earlier, totals)
    lfn_col = jnp.transpose(lfn_ref[...])[:SUBLANES, :]
    ck_new = (ck[rows - SUBLANES:, page - 1:page] + lfn_col) * LOG2E
    r_e = lax.broadcasted_iota(jnp.int32, (page, wide), 0)
    c_e = lax.broadcasted_iota(jnp.int32, (page, wide), 1)
    spread = jnp.where((c_e // N_HEADS) == r_e, 1.0, 0.0).astype(BF16)
    ck_wide = _dot_sel_rhs(ck * LOG2E, spread)

    row8 = lax.broadcasted_iota(jnp.int32, (SUBLANES, wide), 0)
    lane = lax.broadcasted_iota(jnp.int32, (SUBLANES, wide), 1)
    own_head = (lane % N_HEADS) == row8

    s_new = jnp.sum(q4 * kn4, axis=1, keepdims=True) - ck_new
    per_group = pl.cdiv(n_pages, DECODE_PAGE_GROUPS)
    parts = [(s_new, jnp.ones_like(s_new), vn4)]
    for g0 in range(0, n_pages, per_group):
        pages = range(g0, min(g0 + per_group, n_pages))
        logits = [jnp.where(own_head,
                            _dot_nt(q4b, k_refs[j][...].astype(BF16)) - ck_wide[j * SUBLANES:(j + 1) * SUBLANES],
                            MASK_VALUE) for j in pages]
        m = functools.reduce(jnp.maximum, [jnp.max(s, axis=1, keepdims=True) for s in logits])
        l = jnp.zeros_like(m)
        acc = jnp.zeros((SUBLANES, HEAD_W), F32)
        for j, s in zip(pages, logits):
            p = jnp.exp2(s - m)
            l = l + jnp.sum(p, axis=1, keepdims=True)
            acc = acc + _dot(p.astype(BF16), v_refs[j][...].astype(BF16))
        parts.append((m, l, acc))
    m_all = functools.reduce(jnp.maximum, [m for m, _, _ in parts])
    l_all = sum(jnp.exp2(m - m_all) * l for m, l, _ in parts)
    out = sum(jnp.exp2(m - m_all) * acc for m, _, acc in parts) / l_all
    o_ref[...] = jnp.concatenate([out[h:h + 1, :] for h in range(N_HEADS)], axis=1).astype(o_ref.dtype)


def _fox_decode(layer, page_table, qc, kn, vn, lfn, cache_k, cache_v, cache_lf):
    n, n_pages = page_table.shape
    width = qc.shape[1]
    page = cache_lf.shape[3]

    vec_spec = pl.BlockSpec((None, 1, width), lambda b, pt: (b, 0, 0))
    new_spec = pl.BlockSpec((None, N_HEADS, HEAD_W), lambda b, pt: (b, 0, 0))

    def page_specs(block):
        return [pl.BlockSpec((None, None) + block, functools.partial(
            lambda b, pt, j: (layer, pt[b, j], 0, 0), j=j)) for j in range(n_pages)]

    grid_spec = pltpu.PrefetchScalarGridSpec(
        num_scalar_prefetch=1, grid=(n,),
        in_specs=[vec_spec, new_spec, new_spec, pl.BlockSpec((None, 1, LANES), lambda b, pt: (b, 0, 0))]
        + page_specs((page * N_HEADS, HEAD_W)) + page_specs((page * N_HEADS, HEAD_W))
        + page_specs((N_HEADS, page)),
        out_specs=vec_spec)
    out = pl.pallas_call(
        functools.partial(_fox_decode_kernel, n_pages=n_pages),
        grid_spec=grid_spec,
        out_shape=jax.ShapeDtypeStruct((n, 1, width), BF16),
        compiler_params=_params(("arbitrary",), 48),
        name="fox_attention_decode",
    )(page_table, qc.reshape(n, 1, width), kn.reshape(n, N_HEADS, HEAD_W), vn.reshape(n, N_HEADS, HEAD_W),
      lfn.reshape(n, 1, LANES),
      *([cache_k] * n_pages), *([cache_v] * n_pages), *([cache_lf] * n_pages))
    return out.reshape(n, width)


def _merge_kernel(x_ref, shift_ref, scale_ref, gate_ref, oa_ref, ob_ref, oc_ref, sga_ref, sgb_ref, sgc_ref,
                  wmg_ref, wbr_ref, wout_ref, gna_ref, gnb_ref, lng_ref, lnb_ref, o_ref, *, alpha):
    x = x_ref[...]
    d = x.shape[1]
    u = (x * (1.0 + scale_ref[...]) + shift_ref[...]).astype(BF16)

    def head_rmsnorm(o, gain):
        parts = []
        for h in range(N_HEADS):
            blk = o[:, h * HEAD_W:(h + 1) * HEAD_W]
            parts.append(blk * lax.rsqrt(jnp.mean(blk * blk, axis=1, keepdims=True) + RMS_EPS))
        return jnp.concatenate(parts, axis=1) * gain

    branches = (
        head_rmsnorm(oa_ref[...].astype(F32), gna_ref[...]) * sga_ref[...].astype(F32),
        head_rmsnorm(ob_ref[...].astype(F32), gnb_ref[...]) * sgb_ref[...].astype(F32),
        oc_ref[...].astype(F32) * sgc_ref[...].astype(F32))
    merged = jnp.zeros(x.shape, F32)
    for n in range(N_BRANCH):
        gate_n = _sigmoid(_dot(u, wmg_ref[:, n * d:(n + 1) * d]))
        merged = merged + gate_n * _dot(branches[n].astype(BF16), wbr_ref[n])
    y = _dot(merged.astype(BF16), wout_ref[...])
    r = alpha * x + gate_ref[...] * y
    mu = jnp.mean(r, axis=1, keepdims=True)
    rc = r - mu
    var = jnp.mean(rc * rc, axis=1, keepdims=True)
    o_ref[...] = rc * lax.rsqrt(var + LN_EPS) * lng_ref[...] + lnb_ref[...]


def _merge(x, shift, scale, gate, o_a, o_b, o_c, sga, sgb, sgc, wmg_all, wbr_all, wout_all, gna_l, gnb_l,
           lng_l, lnb_l, layer, rows_per_seq, alpha):
    t, d = x.shape
    tm, mod_spec = _row_tiling(t, d, rows_per_seq)
    row_spec = pl.BlockSpec((tm, d), lambda i: (i, 0))
    br_spec = pl.BlockSpec((tm, _BW), lambda i: (i, 0))
    in_specs = [row_spec, mod_spec, mod_spec, mod_spec] + [br_spec] * 6 + [
        _resident((None,) + wmg_all.shape[1:], lambda i: (layer, 0, 0)),
        _resident((None,) + wbr_all.shape[1:], lambda i: (layer, 0, 0, 0)),
        _resident((None,) + wout_all.shape[1:], lambda i: (layer, 0, 0)),
        _resident(gna_l.shape, lambda i: (0, 0)), _resident(gnb_l.shape, lambda i: (0, 0)),
        _resident(lng_l.shape, lambda i: (0, 0)), _resident(lnb_l.shape, lambda i: (0, 0))]
    return pl.pallas_call(
        functools.partial(_merge_kernel, alpha=alpha),
        grid=(t // tm,), in_specs=in_specs, out_specs=row_spec,
        out_shape=jax.ShapeDtypeStruct((t, d), F32),
        compiler_params=_params(("arbitrary",), 56), name="merge_out_projection",
    )(x, shift, scale, gate, o_a, o_b, o_c, sga, sgb, sgc, wmg_all, wbr_all, wout_all, gna_l, gnb_l,
      lng_l, lnb_l)


def _permute_in_weights(w_in):
    widths = (N_HEADS * DK_GLA, N_HEADS * DK_GLA, _BW, _BW, GLA_RANK,
              _BW, _BW, _BW, _BW, _BW, _BW, _BW, N_HEADS, _BW)
    names = ("qa", "ka", "va", "ga", "ra", "qb", "fb", "ib", "gb", "qc", "kc", "vc", "fc", "gc")
    cols, off = {}, 0
    for name, w in zip(names, widths):
        cols[name] = w_in[:, :, off:off + w]
        off += w
    mg = w_in[:, :, off:]
    pad = jnp.zeros(w_in.shape[:2] + (LANES - N_HEADS - GLA_RANK,), w_in.dtype)
    order = [cols[n] for n in ("qa", "ka", "va", "ga", "qb", "fb", "ib", "gb", "qc", "kc", "vc", "gc")]
    order += [cols["fc"], cols["ra"], pad]
    return jnp.concatenate(order, axis=2).astype(BF16), mg.astype(BF16)


def kernel(x_prompt, x_sample, cache_fox_k, cache_fox_v, cache_fox_logf, state_gla, state_hgrn, page_table,
           c_prompt, c_sample, w_in, w_gla_a2, b_gla_a2, norm_gla, hgrn_lb_logits, norm_hgrn, b_fox_f,
           w_branch, w_out, w_ada, b_ada, ln_g, ln_b):
    batch, seq, d = x_prompt.shape
    n_dec = x_sample.shape[0]
    depth = w_in.shape[0]
    n_phys, page = cache_fox_k.shape[1], cache_fox_k.shape[2]
    alpha = (2 * depth) ** 0.25

    w_proj, w_mg = _permute_in_weights(w_in)
    wa2 = jnp.zeros((depth, LANES, N_HEADS * DK_GLA), F32).at[:, N_HEADS:N_HEADS + GLA_RANK, :].set(w_gla_a2)
    wa2 = wa2.astype(BF16)
    ba2 = b_gla_a2.reshape(depth, 1, -1)
    bf_pad = jnp.zeros((depth, 1, LANES), F32).at[:, 0, :N_HEADS].set(b_fox_f)
    w_br = w_branch.astype(BF16)
    w_o = w_out.astype(BF16)
    gna = norm_gla.reshape(depth, 1, -1)
    gnb = norm_hgrn.reshape(depth, 1, -1)
    lng = ln_g.reshape(depth, 1, -1)
    lnb = ln_b.reshape(depth, 1, -1)
    cache_k = cache_fox_k.reshape(depth, n_phys, page * N_HEADS, HEAD_W)
    cache_v = cache_fox_v.reshape(depth, n_phys, page * N_HEADS, HEAD_W)
    cache_lf = jnp.swapaxes(cache_fox_logf, 2, 3)

    lower = _lower_bounds(hgrn_lb_logits).reshape(depth, 1, -1)
    mod = _modulation(jnp.concatenate([c_prompt, c_sample], axis=0), w_ada, b_ada)

    xp = x_prompt.reshape(batch * seq, d)
    xs = x_sample.reshape(n_dec, d)
    outs = {k: [] for k in ("gla_p", "gla_s", "hg_p", "hg_s", "kp", "vp", "fp", "ks", "vs", "fs")}
    for l in range(depth):
        mod_p = mod[l, :batch].reshape(batch, 1, 3 * d)
        shift_p, scale_p, gate_p = mod_p[:, :, :d], mod_p[:, :, d:2 * d], mod_p[:, :, 2 * d:]
        mod_s = mod[l, batch:]
        shift_s, scale_s, gate_s = mod_s[:, :d], mod_s[:, d:2 * d], mod_s[:, 2 * d:]

        z = _inproj(xp, shift_p, scale_p, w_proj, wa2[l], ba2[l], lower[l], bf_pad[l], l, seq)
        o_a, s_a = _scan_prompt(z["qa"], z["ka"], z["va"], z["ga"], batch, seq, DK_GLA)
        o_b, s_b = _scan_prompt(z["sqb"], z["kb"], z["ib"], z["lf"], batch, seq, HEAD_W)
        cum_t = _forget_cumsum(z["zs"], batch, seq)
        o_c = _fox_attn_prompt(z["qc"], z["kc"], z["vc"], cum_t, batch, seq)
        xp = _merge(xp, shift_p, scale_p, gate_p, o_a, o_b, o_c, z["sga"], z["sgb"], z["sgc"],
                    w_mg, w_br, w_o, gna[l], gnb[l], lng[l], lnb[l], l, seq, alpha)
        outs["gla_p"].append(s_a)
        outs["hg_p"].append(s_b)
        outs["kp"].append(z["kc"])
        outs["vp"].append(z["vc"])
        outs["fp"].append(z["logf"])

        z = _inproj(xs, shift_s, scale_s, w_proj, wa2[l], ba2[l], lower[l], bf_pad[l], l, 1)
        o_a, s_a = _scan_step(z["qa"], z["ka"], z["va"], z["ga"], state_gla, l)
        o_b, s_b = _scan_step(z["sqb"], z["kb"], z["ib"], z["lf"], state_hgrn, l)
        o_c = _fox_decode(l, page_table, z["qc"], z["kc"], z["vc"], z["zs"], cache_k, cache_v, cache_lf)
        xs = _merge(xs, shift_s, scale_s, gate_s, o_a, o_b, o_c, z["sga"], z["sgb"], z["sgc"],
                    w_mg, w_br, w_o, gna[l], gnb[l], lng[l], lnb[l], l, 1, alpha)
        outs["gla_s"].append(s_a)
        outs["hg_s"].append(s_b)
        outs["ks"].append(z["kc"])
        outs["vs"].append(z["vc"])
        outs["fs"].append(z["logf"])

    def stack(name, shape):
        return jnp.stack(outs[name]).reshape((depth,) + shape)

    return (xp.reshape(batch, seq, d), xs.reshape(n_dec, 1, d),
            jnp.stack(outs["gla_p"]), jnp.stack(outs["gla_s"]),
            jnp.stack(outs["hg_p"]), jnp.stack(outs["hg_s"]),
            stack("kp", (batch, seq, N_HEADS, HEAD_W)), stack("vp", (batch, seq, N_HEADS, HEAD_W)),
            stack("fp", (batch, seq, N_HEADS)),
            stack("ks", (n_dec, 1, N_HEADS, HEAD_W)), stack("vs", (n_dec, 1, N_HEADS, HEAD_W)),
            stack("fs", (n_dec, 1, N_HEADS)))
```

```python
import functools

import jax
import jax.numpy as jnp
from jax import lax
from jax.experimental import pallas as pl
from jax.experimental.pallas import tpu as pltpu

F32 = jnp.float32
BF16 = jnp.bfloat16

N_HEADS = 4
DK_GLA = 64
HEAD_W = 128
GLA_RANK = 16
GLA_GATE_NORM = 16.0
N_BRANCH = 3
LN_EPS = 1e-5
RMS_EPS = 1e-6
F_FLOOR = 1e-30
MASK_VALUE = -1e30
LOG2E = 1.4426950408889634

LANES = 128
SUBLANES = 8

ROWS_INPROJ = 512
SCAN_CHUNK = 64
SCAN_ROWS = 512
ATTN_BLOCK = 256
ATTN_Q_ROWS = 256
DECODE_ROWS = 8
DECODE_PAGE_GROUPS = 4
SCAN_SAFE_EXP = 80.0

_BW = N_HEADS * HEAD_W
_GROUPS = ("qka", "va", "ga", "qb", "fb", "ib", "gb", "qc", "kc", "vc", "gc", "small")
_WIDTHS = dict(qka=_BW, va=_BW, ga=_BW, qb=_BW, fb=_BW, ib=_BW, gb=_BW, qc=_BW, kc=_BW, vc=_BW,
               gc=_BW, small=LANES)
_OFFSETS = {}
_acc = 0
for _g in _GROUPS:
    _OFFSETS[_g] = _acc
    _acc += _WIDTHS[_g]
N_PROJ = _acc


def _dot(a, b):
    return jnp.dot(a, b, preferred_element_type=F32)


def _dot_nt(a, b):
    return lax.dot_general(a, b, (((1,), (1,)), ((), ())), preferred_element_type=F32)


def _dot_tn(a, b):
    return lax.dot_general(a, b, (((0,), (0,)), ((), ())), preferred_element_type=F32)


def _split3(x):
    x1 = x.astype(BF16)
    r1 = x - x1.astype(F32)
    x2 = r1.astype(BF16)
    x3 = (r1 - x2.astype(F32)).astype(BF16)
    return x1, x2, x3


def _dot_sel_lhs(sel, x):
    x1, x2, x3 = _split3(x)
    return (_dot(sel, x1) + _dot(sel, x2)) + _dot(sel, x3)


def _dot_sel_rhs(x, sel):
    x1, x2, x3 = _split3(x)
    return (_dot(x1, sel) + _dot(x2, sel)) + _dot(x3, sel)


def _log_sigmoid(x):
    return jnp.minimum(x, 0.0) - jnp.log1p(jnp.exp(-jnp.abs(x)))


def _sigmoid(x):
    return 1.0 / (1.0 + jnp.exp(-x))


def _silu(x):
    return x * _sigmoid(x)


def _lower_tri(n):
    r = lax.broadcasted_iota(jnp.int32, (n, n), 0)
    c = lax.broadcasted_iota(jnp.int32, (n, n), 1)
    return jnp.where(r >= c, 1.0, 0.0).astype(BF16)


def _params(semantics, vmem_mb):
    return pltpu.CompilerParams(dimension_semantics=semantics, vmem_limit_bytes=vmem_mb * 1024 * 1024)


def _resident(block_shape, index_map):
    return pl.BlockSpec(block_shape, index_map, pipeline_mode=pl.Buffered(1))


def _row_tiling(t, d, rows_per_seq):
    if rows_per_seq == 1:
        tm = min(ROWS_INPROJ, t)
        return tm, pl.BlockSpec((tm, d), lambda i: (i, 0))
    tm = min(ROWS_INPROJ, rows_per_seq)
    tiles_per_seq = rows_per_seq // tm
    return tm, pl.BlockSpec((None, 1, d), lambda i: (i // tiles_per_seq, 0, 0))


def _lower_bound_kernel(x_ref, o_ref):
    x = x_ref[...]
    depth = x.shape[0]
    e = jnp.exp(x - jnp.max(x, axis=0, keepdims=True))
    p = e / jnp.sum(e, axis=0, keepdims=True)
    acc = jnp.zeros_like(p[0:1])
    for l in range(depth):
        acc = acc + p[l:l + 1]
        o_ref[l:l + 1, :] = jnp.clip(acc - p[0:1], 0.0, 1.0)


def _lower_bounds(logits):
    return pl.pallas_call(
        _lower_bound_kernel, out_shape=jax.ShapeDtypeStruct(logits.shape, F32), name="hgrn_lower_bounds",
    )(logits.astype(F32))


def _mod_kernel(c_ref, w_ref, b_ref, o_ref):
    c = c_ref[...]
    o_ref[...] = _dot(_silu(c).astype(BF16), w_ref[...].astype(BF16)) + b_ref[...]


def _modulation(c_all, w_ada, b_ada):
    depth, d, n3 = w_ada.shape
    rows = c_all.shape[0]
    tn = 512
    return pl.pallas_call(
        _mod_kernel,
        grid=(depth, n3 // tn),
        in_specs=[pl.BlockSpec((rows, d), lambda l, j: (0, 0)),
                  pl.BlockSpec((None, d, tn), lambda l, j: (l, 0, j)),
                  pl.BlockSpec((None, 1, tn), lambda l, j: (l, 0, j))],
        out_specs=pl.BlockSpec((None, rows, tn), lambda l, j: (l, 0, j)),
        out_shape=jax.ShapeDtypeStruct((depth, rows, n3), F32),
        compiler_params=_params(("arbitrary", "arbitrary"), 32),
        name="adaln_modulation",
    )(c_all, w_ada, b_ada.reshape(depth, 1, n3))


def _inproj_kernel(x_ref, shift_ref, scale_ref, w_ref, wa2_ref, ba2_ref, lb_ref, bf_ref, kc_all_ref, vc_all_ref,
                   qa_ref, ka_ref, va_ref, sga_ref, ga_ref,
                   sqb_ref, lf_ref, kb_ref, ib_ref, sgb_ref,
                   qc_ref, kc_ref, vc_ref, sgc_ref, zs_ref, logf_ref):
    del kc_all_ref, vc_all_ref
    u = (x_ref[...] * (1.0 + scale_ref[...]) + shift_ref[...]).astype(BF16)
    tm = x_ref.shape[0]

    def proj(name):
        off = _OFFSETS[name]
        return _dot(u, w_ref[:, off:off + _WIDTHS[name]])

    zqk = proj("qka")
    half = N_HEADS * DK_GLA
    qa_ref[...] = (zqk[:, :half] * (DK_GLA ** -0.5)).astype(qa_ref.dtype)
    ka_ref[...] = zqk[:, half:].astype(ka_ref.dtype)
    va_ref[...] = proj("va").astype(va_ref.dtype)
    sga_ref[...] = _silu(proj("ga")).astype(sga_ref.dtype)
    zs = proj("small")
    pre = _dot(zs.astype(BF16), wa2_ref[...]) + ba2_ref[...]
    ga_ref[...] = _log_sigmoid(pre) * (1.0 / GLA_GATE_NORM)
    lane = lax.broadcasted_iota(jnp.int32, zs.shape, 1)
    logf = jnp.where(lane < N_HEADS, _log_sigmoid(zs + bf_ref[...]), 0.0)
    zs_ref[...] = logf
    logf_ref[...] = logf[:, :N_HEADS]
    sqb_ref[...] = _silu(proj("qb")).astype(sqb_ref.dtype)
    fb = proj("fb")
    lb = lb_ref[...]
    f = lb + (1.0 - lb) * _sigmoid(fb)
    lf_ref[...] = jnp.log(jnp.maximum(f, F_FLOOR))
    kb_ref[...] = ((1.0 - lb) * _sigmoid(-fb)).astype(kb_ref.dtype)
    ib_ref[...] = proj("ib").astype(ib_ref.dtype)
    sgb_ref[...] = _silu(proj("gb")).astype(sgb_ref.dtype)
    qc_ref[...] = (proj("qc") * (HEAD_W ** -0.5 * LOG2E)).astype(qc_ref.dtype)
    for name, ref in (("kc", kc_ref), ("vc", vc_ref)):
        z = proj(name)
        for h in range(N_HEADS):
            ref[pl.ds(h, tm, stride=N_HEADS), :] = z[:, h * HEAD_W:(h + 1) * HEAD_W]
    sgc_ref[...] = _silu(proj("gc")).astype(sgc_ref.dtype)


_INPROJ_OUTS = (
    ("qa", 1, N_HEADS * DK_GLA, BF16), ("ka", 1, N_HEADS * DK_GLA, BF16), ("va", 1, _BW, BF16),
    ("sga", 1, _BW, BF16), ("ga", 1, N_HEADS * DK_GLA, F32),
    ("sqb", 1, _BW, BF16), ("lf", 1, _BW, F32), ("kb", 1, _BW, BF16), ("ib", 1, _BW, BF16),
    ("sgb", 1, _BW, BF16),
    ("qc", 1, _BW, BF16), ("kc", N_HEADS, HEAD_W, F32), ("vc", N_HEADS, HEAD_W, F32), ("sgc", 1, _BW, BF16),
    ("zs", 1, LANES, F32), ("logf", 1, N_HEADS, F32))


def _inproj(x, shift, scale, w_all, wa2_l, ba2_l, lb_l, bf_l, kc_all, vc_all, layer, rows_per_seq):
    t, d = x.shape
    tm, mod_spec = _row_tiling(t, d, rows_per_seq)
    names = [name for name, _, _, _ in _INPROJ_OUTS]
    in_specs = [pl.BlockSpec((tm, d), lambda i: (i, 0)), mod_spec, mod_spec,
                _resident((None, d, N_PROJ), lambda i: (layer, 0, 0)),
                _resident(wa2_l.shape, lambda i: (0, 0)),
                _resident(ba2_l.shape, lambda i: (0, 0)),
                _resident(lb_l.shape, lambda i: (0, 0)),
                _resident(bf_l.shape, lambda i: (0, 0)),
                pl.BlockSpec(memory_space=pl.ANY), pl.BlockSpec(memory_space=pl.ANY)]
    out_specs, out_shape = [], []
    for name, r, w, dt in _INPROJ_OUTS:
        if name in ("kc", "vc"):
            out_specs.append(pl.BlockSpec((None, tm * r, w), lambda i: (layer, i, 0)))
            out_shape.append(jax.ShapeDtypeStruct(kc_all.shape, dt))
        else:
            out_specs.append(pl.BlockSpec((tm * r, w), lambda i: (i, 0)))
            out_shape.append(jax.ShapeDtypeStruct((t * r, w), dt))
    outs = pl.pallas_call(
        _inproj_kernel, grid=(t // tm,), in_specs=in_specs, out_specs=out_specs, out_shape=out_shape,
        input_output_aliases={8: names.index("kc"), 9: names.index("vc")},
        compiler_params=_params(("arbitrary",), 56), name="in_projection",
    )(x, shift, scale, w_all, wa2_l, ba2_l, lb_l, bf_l, kc_all, vc_all)
    return dict(zip(names, outs))


def _scan_kernel(q_ref, k_ref, v_ref, g_ref, o_ref, s_ref, st_scr, kf_scr, bf_scr, *, sub, chunk):
    step = pl.program_id(1)
    n_steps = pl.num_programs(1)
    n_chunks = q_ref.shape[0] // chunk
    n_units = q_ref.shape[1] // LANES
    dk = LANES // sub

    @pl.when(step == 0)
    def _():
        st_scr[...] = jnp.zeros_like(st_scr)

    tri = _lower_tri(chunk)
    row_cc = lax.broadcasted_iota(jnp.int32, (chunk, chunk), 0)
    col_cc = lax.broadcasted_iota(jnp.int32, (chunk, chunk), 1)
    causal = row_cc >= col_cc
    lane = lax.broadcasted_iota(jnp.int32, (chunk, LANES), 1)
    mid = chunk // 2

    cums, worst = [], jnp.zeros((1, 1), F32)
    for c in range(n_chunks):
        b = _dot_sel_lhs(tri, g_ref[c * chunk:(c + 1) * chunk, :])
        cums.append(b)
        worst = jnp.maximum(worst, jnp.max(jnp.abs(b - b[mid - 1:mid, :]), keepdims=True))
    fast = worst[0, 0] <= SCAN_SAFE_EXP

    def direct_scores(q, k_h, b):
        kf_scr[...] = k_h
        bf_scr[...] = b

        def body(s, acc):
            k_row = kf_scr[pl.ds(s, 1), :]
            b_row = bf_scr[pl.ds(s, 1), :]
            w = q * k_row * jnp.exp(jnp.minimum(b - b_row, 0.0))
            return jnp.where(col_cc == s, jnp.sum(w, axis=1, keepdims=True), acc)

        return lax.fori_loop(0, chunk, body, jnp.zeros((chunk, chunk), F32))

    def run(factorised):
        st = [st_scr[h] for h in range(N_HEADS)]
        for c in range(n_chunks):
            rs = slice(c * chunk, (c + 1) * chunk)
            for u in range(n_units):
                sl = slice(u * LANES, (u + 1) * LANES)
                b = cums[c][:, sl]
                b_end = b[chunk - 1:chunk, :]
                a = b - b[mid - 1:mid, :]
                q = q_ref[rs, sl].astype(F32)
                k = k_ref[rs, sl].astype(F32)
                q_in = (q * jnp.exp(b)).astype(BF16)
                k_out = k * jnp.exp(b_end - b)
                decay_end = jnp.exp(b_end)
                if factorised:
                    qs = (q * jnp.exp(a)).astype(BF16)
                    ks = k * jnp.exp(-a)
                for j in range(sub):
                    h = u * sub + j
                    hs = slice(h * HEAD_W, (h + 1) * HEAD_W)
                    in_head = None if sub == 1 else (lane // dk) == j

                    def own(x, in_head=in_head):
                        return x if in_head is None else jnp.where(in_head, x, 0.0)

                    if factorised:
                        scores = _dot_nt(qs, own(ks).astype(BF16))
                    else:
                        scores = direct_scores(q, own(k), b)
                    scores = jnp.where(causal, scores, 0.0)
                    v = v_ref[rs, hs]
                    o = _dot(scores.astype(BF16), v) + _dot_nt(q_in, st[h].astype(BF16))
                    o_ref[rs, hs] = o.astype(o_ref.dtype)
                    st[h] = st[h] * decay_end + _dot_tn(v, own(k_out).astype(BF16))
        for h in range(N_HEADS):
            st_scr[h] = st[h]

    @pl.when(fast)
    def _():
        run(True)

    @pl.when(jnp.logical_not(fast))
    def _():
        run(False)

    @pl.when(step == n_steps - 1)
    def _():
        for h in range(N_HEADS):
            j = h % sub
            s_full = jnp.transpose(st_scr[h])
            s_ref[h] = s_full[j * dk:(j + 1) * dk, :]


def _scan_prompt(q, k, v, g, batch, seq, dk):
    t = q.shape[0]
    sub = LANES // dk
    rows = min(SCAN_ROWS, seq)
    chunk = min(SCAN_CHUNK, rows)
    n_steps = seq // rows
    wq = q.shape[1]

    def at(b, c):
        return (b * n_steps + c, 0)

    o, s = pl.pallas_call(
        functools.partial(_scan_kernel, sub=sub, chunk=chunk),
        grid=(batch, n_steps),
        in_specs=[pl.BlockSpec((rows, wq), at), pl.BlockSpec((rows, wq), at),
                  pl.BlockSpec((rows, _BW), at), pl.BlockSpec((rows, wq), at)],
        out_specs=[pl.BlockSpec((rows, _BW), at),
                   pl.BlockSpec((None, N_HEADS, dk, HEAD_W), lambda b, c: (b, 0, 0, 0))],
        out_shape=[jax.ShapeDtypeStruct((t, _BW), BF16),
                   jax.ShapeDtypeStruct((batch, N_HEADS, dk, HEAD_W), F32)],
        scratch_shapes=[pltpu.VMEM((N_HEADS, HEAD_W, LANES), F32),
                        pltpu.VMEM((chunk, LANES), F32), pltpu.VMEM((chunk, LANES), F32)],
        compiler_params=_params(("arbitrary", "arbitrary"), 32),
        name="linear_scan_prompt",
    )(q, k, v, g)
    return o, s


def _scan_step_kernel(q_ref, k_ref, v_ref, g_ref, s_ref, so_all_ref, o_ref, so_ref):
    del so_all_ref
    rows = q_ref.shape[0]
    dk = s_ref.shape[2]
    for i in range(rows):
        for h in range(N_HEADS):
            ks = slice(h * dk, (h + 1) * dk)
            vs = slice(h * HEAD_W, (h + 1) * HEAD_W)
            decay = jnp.transpose(jnp.exp(g_ref[i:i + 1, ks]))
            k_col = jnp.transpose(k_ref[i:i + 1, ks].astype(F32))
            q_col = jnp.transpose(q_ref[i:i + 1, ks].astype(F32))
            v_row = v_ref[i:i + 1, vs].astype(F32)
            s_new = decay * s_ref[i, h] + k_col * v_row
            so_ref[i, h] = s_new
            o_ref[i:i + 1, vs] = jnp.sum(q_col * s_new, axis=0, keepdims=True).astype(o_ref.dtype)


def _scan_step(q, k, v, g, state_all, new_state_all, layer):
    _, n, _, dk, dv = state_all.shape
    rows = min(DECODE_ROWS, n)
    wq = q.shape[1]
    state_spec = pl.BlockSpec((None, rows, N_HEADS, dk, dv), lambda i: (layer, i, 0, 0, 0))
    return pl.pallas_call(
        _scan_step_kernel,
        grid=(n // rows,),
        in_specs=[pl.BlockSpec((rows, wq), lambda i: (i, 0)), pl.BlockSpec((rows, wq), lambda i: (i, 0)),
                  pl.BlockSpec((rows, _BW), lambda i: (i, 0)), pl.BlockSpec((rows, wq), lambda i: (i, 0)),
                  state_spec, pl.BlockSpec(memory_space=pl.ANY)],
        out_specs=[pl.BlockSpec((rows, _BW), lambda i: (i, 0)), state_spec],
        out_shape=[jax.ShapeDtypeStruct((n, _BW), BF16), jax.ShapeDtypeStruct(state_all.shape, F32)],
        input_output_aliases={5: 1},
        compiler_params=_params(("arbitrary",), 32),
        name="linear_scan_step",
    )(q, k, v, g, state_all, new_state_all)


def _forget_cumsum_kernel(z_ref, o_ref, carry_scr):
    @pl.when(pl.program_id(1) == 0)
    def _():
        carry_scr[...] = jnp.zeros_like(carry_scr)

    x = z_ref[...]
    n = x.shape[0]
    cum = _dot_sel_lhs(_lower_tri(n), x) + carry_scr[...]
    carry_scr[...] = cum[n - 1:n, :]
    o_ref[...] = jnp.transpose(cum * LOG2E)[:SUBLANES, :]


def _forget_cumsum(zs, batch, seq):
    blk = min(ATTN_BLOCK, seq)
    nb = seq // blk
    return pl.pallas_call(
        _forget_cumsum_kernel,
        grid=(batch, nb),
        in_specs=[pl.BlockSpec((blk, LANES), lambda b, c: (b * nb + c, 0))],
        out_specs=pl.BlockSpec((None, None, SUBLANES, blk), lambda b, c: (b, c, 0, 0)),
        out_shape=jax.ShapeDtypeStruct((batch, nb, SUBLANES, blk), F32),
        scratch_shapes=[pltpu.VMEM((1, LANES), F32)],
        compiler_params=_params(("arbitrary", "arbitrary"), 32),
        name="forget_cumsum",
    )(zs)


def _fox_attn_kernel(q_ref, k_ref, v_ref, cum_ref, o_ref, k_scr, v_scr):
    qi = pl.program_id(1)
    tq = q_ref.shape[0]
    tk = cum_ref.shape[2]
    seq = k_scr.shape[1]

    @pl.when(qi == 0)
    def _():
        for h in range(N_HEADS):
            k_scr[h] = k_ref[pl.ds(h, seq, stride=N_HEADS), :].astype(BF16)
            v_scr[h] = v_ref[pl.ds(h, seq, stride=N_HEADS), :].astype(BF16)

    q = [q_ref[:, h * HEAD_W:(h + 1) * HEAD_W] for h in range(N_HEADS)]
    n_full = (qi * tq) // tk

    def block(ki, carry, masked):
        rows = pl.ds(pl.multiple_of(ki * tk, tk), tk)
        out = []
        for h in range(N_HEADS):
            m, l, acc = carry[h]
            s = _dot_nt(q[h], k_scr[h, rows, :]) - cum_ref[ki, h:h + 1, :]
            if masked:
                r = lax.broadcasted_iota(jnp.int32, s.shape, 0) + qi * tq
                c = lax.broadcasted_iota(jnp.int32, s.shape, 1) + ki * tk
                s = jnp.where(r >= c, s, MASK_VALUE)
            m_new = jnp.maximum(m, jnp.max(s, axis=1, keepdims=True))
            p = jnp.exp2(s - m_new)
            alpha = jnp.exp2(m - m_new)
            l = alpha * l + jnp.sum(p, axis=1, keepdims=True)
            acc = alpha * acc + _dot(p.astype(BF16), v_scr[h, rows, :])
            out.append((m_new, l, acc))
        return tuple(out)

    init = tuple((jnp.full((tq, 1), MASK_VALUE, F32), jnp.zeros((tq, 1), F32), jnp.zeros((tq, HEAD_W), F32))
                 for _ in range(N_HEADS))
    carry = lax.fori_loop(0, n_full, lambda ki, c: block(ki, c, False), init)
    final = block(n_full, carry, True)
    for h in range(N_HEADS):
        _, l, acc = final[h]
        o_ref[:, h * HEAD_W:(h + 1) * HEAD_W] = (acc / l).astype(o_ref.dtype)


def _fox_attn_prompt(qc, kc_all, vc_all, cum_t, layer, batch, seq):
    t = qc.shape[0]
    blk = min(ATTN_BLOCK, seq)
    nb = seq // blk
    tq = min(ATTN_Q_ROWS, blk)
    nq = seq // tq
    kv_spec = pl.BlockSpec((None, seq * N_HEADS, HEAD_W), lambda b, i: (layer, b, 0))
    return pl.pallas_call(
        _fox_attn_kernel,
        grid=(batch, nq),
        in_specs=[pl.BlockSpec((tq, _BW), lambda b, i: (b * nq + i, 0)), kv_spec, kv_spec,
                  pl.BlockSpec((None, nb, SUBLANES, blk), lambda b, i: (b, 0, 0, 0))],
        out_specs=pl.BlockSpec((tq, _BW), lambda b, i: (b * nq + i, 0)),
        out_shape=jax.ShapeDtypeStruct((t, _BW), BF16),
        scratch_shapes=[pltpu.VMEM((N_HEADS, seq, HEAD_W), BF16), pltpu.VMEM((N_HEADS, seq, HEAD_W), BF16)],
        compiler_params=_params(("arbitrary", "arbitrary"), 40),
        name="fox_attention_prompt",
    )(qc, kc_all, vc_all, cum_t)


def _fox_decode_kernel(pt_ref, q_ref, kn_ref, vn_ref, lfn_ref, ck_hbm, cv_hbm, clf_hbm, o_ref,
                       k_buf, v_buf, lf_buf, sems, *, layer):
    b = pl.program_id(0)
    n_seq = pl.num_programs(0)
    n_pages = k_buf.shape[1]
    slot = b % 2

    def page_copies(seq, slot):
        out = []
        for j in range(n_pages):
            pg = pt_ref[seq, j]
            out.append(pltpu.make_async_copy(ck_hbm.at[layer, pg], k_buf.at[slot, j], sems.at[slot, 0]))
            out.append(pltpu.make_async_copy(cv_hbm.at[layer, pg], v_buf.at[slot, j], sems.at[slot, 1]))
            out.append(pltpu.make_async_copy(clf_hbm.at[layer, pg], lf_buf.at[slot, j], sems.at[slot, 2]))
        return out

    @pl.when(b == 0)
    def _():
        for c in page_copies(0, 0):
            c.start()

    for c in page_copies(b, slot):
        c.wait()
    for c in page_copies(jnp.minimum(b + 1, n_seq - 1), 1 - slot):
        c.start()

    k_refs = [k_buf.at[slot, j] for j in range(n_pages)]
    v_refs = [v_buf.at[slot, j] for j in range(n_pages)]
    lf_refs = [lf_buf.at[slot, j] for j in range(n_pages)]
    page = lf_buf.shape[3]
    wide = page * N_HEADS
    rows = n_pages * SUBLANES
    pad = jnp.zeros((SUBLANES - N_HEADS, HEAD_W), F32)

    def head_rows(x):
        return jnp.concatenate([x[:, h * HEAD_W:(h + 1) * HEAD_W] for h in range(N_HEADS)] + [pad], axis=0)

    q4 = head_rows(q_ref[...].astype(F32))
    q4b = q4.astype(BF16)
    kn4 = jnp.concatenate([kn_ref[...], pad], axis=0)
    vn4 = jnp.concatenate([vn_ref[...], pad], axis=0)

    lf_rows = jnp.concatenate([x for j in range(n_pages) for x in (lf_refs[j][...], pad)], axis=0)
    r_p = lax.broadcasted_iota(jnp.int32, (page, page), 0)
    c_p = lax.broadcasted_iota(jnp.int32, (page, page), 1)
    upto = jnp.where(r_p <= c_p, 1.0, 0.0).astype(BF16)
    cum_page = _dot_sel_rhs(lf_rows, upto)
    totals = jnp.broadcast_to(cum_page[:, page - 1:page], (rows, page))
    r_r = lax.broadcasted_iota(jnp.int32, (rows, rows), 0)
    c_r = lax.broadcasted_iota(jnp.int32, (rows, rows), 1)
    earlier = jnp.where(((r_r % SUBLANES) == (c_r % SUBLANES)) & ((c_r // SUBLANES) < (r_r // SUBLANES)),
                        1.0, 0.0).astype(BF16)
    ck = cum_page + _dot_sel_lhs(earlier, totals)
    lfn_col = jnp.transpose(lfn_ref[...])[:SUBLANES, :]
    ck_new = (ck[rows - SUBLANES:, page - 1:page] + lfn_col) * LOG2E
    r_e = lax.broadcasted_iota(jnp.int32, (page, wide), 0)
    c_e = lax.broadcasted_iota(jnp.int32, (page, wide), 1)
    spread = jnp.where((c_e // N_HEADS) == r_e, 1.0, 0.0).astype(BF16)
    ck_wide = _dot_sel_rhs(ck * LOG2E, spread)

    row8 = lax.broadcasted_iota(jnp.int32, (SUBLANES, wide), 0)
    lane = lax.broadcasted_iota(jnp.int32, (SUBLANES, wide), 1)
    own_head = (lane % N_HEADS) == row8

    s_new = jnp.sum(q4 * kn4, axis=1, keepdims=True) - ck_new
    per_group = pl.cdiv(n_pages, DECODE_PAGE_GROUPS)
    parts = [(s_new, jnp.ones_like(s_new), vn4)]
    for g0 in range(0, n_pages, per_group):
        pages = range(g0, min(g0 + per_group, n_pages))
        logits = [jnp.where(own_head,
                            _dot_nt(q4b, k_refs[j][...].astype(BF16)) - ck_wide[j * SUBLANES:(j + 1) * SUBLANES],
                            MASK_VALUE) for j in pages]
        m = functools.reduce(jnp.maximum, [jnp.max(s, axis=1, keepdims=True) for s in logits])
        l = jnp.zeros_like(m)
        acc = jnp.zeros((SUBLANES, HEAD_W), F32)
        for j, s in zip(pages, logits):
            p = jnp.exp2(s - m)
            l = l + jnp.sum(p, axis=1, keepdims=True)
            acc = acc + _dot(p.astype(BF16), v_refs[j][...].astype(BF16))
        parts.append((m, l, acc))
    m_all = functools.reduce(jnp.maximum, [m for m, _, _ in parts])
    l_all = sum(jnp.exp2(m - m_all) * l for m, l, _ in parts)
    out = sum(jnp.exp2(m - m_all) * acc for m, _, acc in parts) / l_all
    o_ref[...] = jnp.concatenate([out[h:h + 1, :] for h in range(N_HEADS)], axis=1).astype(o_ref.dtype)

    @pl.when(b == n_seq - 1)
    def _():
        for c in page_copies(b, 1 - slot):
            c.wait()


def _fox_decode(layer, page_table, qc, kn_all, vn_all, lfn, cache_k, cache_v, cache_lf):
    n, n_pages = page_table.shape
    width = qc.shape[1]
    page = cache_lf.shape[3]
    depth = kn_all.shape[0]

    vec_spec = pl.BlockSpec((None, 1, width), lambda b, pt: (b, 0, 0))
    new_spec = pl.BlockSpec((None, None, N_HEADS, HEAD_W), lambda b, pt: (layer, b, 0, 0))
    hbm_spec = pl.BlockSpec(memory_space=pl.ANY)
    grid_spec = pltpu.PrefetchScalarGridSpec(
        num_scalar_prefetch=1, grid=(n,),
        in_specs=[vec_spec, new_spec, new_spec, pl.BlockSpec((None, 1, LANES), lambda b, pt: (b, 0, 0)),
                  hbm_spec, hbm_spec, hbm_spec],
        out_specs=vec_spec,
        scratch_shapes=[pltpu.VMEM((2, n_pages, page * N_HEADS, HEAD_W), F32),
                        pltpu.VMEM((2, n_pages, page * N_HEADS, HEAD_W), F32),
                        pltpu.VMEM((2, n_pages, N_HEADS, page), F32),
                        pltpu.SemaphoreType.DMA((2, 3))])
    out = pl.pallas_call(
        functools.partial(_fox_decode_kernel, layer=layer),
        grid_spec=grid_spec,
        out_shape=jax.ShapeDtypeStruct((n, 1, width), BF16),
        compiler_params=_params(("arbitrary",), 48),
        name="fox_attention_decode",
    )(page_table, qc.reshape(n, 1, width), kn_all.reshape(depth, n, N_HEADS, HEAD_W),
      vn_all.reshape(depth, n, N_HEADS, HEAD_W), lfn.reshape(n, 1, LANES), cache_k, cache_v, cache_lf)
    return out.reshape(n, width)


def _merge_kernel(x_ref, shift_ref, scale_ref, gate_ref, oa_ref, ob_ref, oc_ref, sga_ref, sgb_ref, sgc_ref,
                  wmg_ref, wbr_ref, wout_ref, gna_ref, gnb_ref, lng_ref, lnb_ref, o_ref, *, alpha):
    x = x_ref[...]
    d = x.shape[1]
    u = (x * (1.0 + scale_ref[...]) + shift_ref[...]).astype(BF16)

    def head_rmsnorm(o, gain):
        parts = []
        for h in range(N_HEADS):
            blk = o[:, h * HEAD_W:(h + 1) * HEAD_W]
            parts.append(blk * lax.rsqrt(jnp.mean(blk * blk, axis=1, keepdims=True) + RMS_EPS))
        return jnp.concatenate(parts, axis=1) * gain

    branches = (
        head_rmsnorm(oa_ref[...].astype(F32), gna_ref[...]) * sga_ref[...].astype(F32),
        head_rmsnorm(ob_ref[...].astype(F32), gnb_ref[...]) * sgb_ref[...].astype(F32),
        oc_ref[...].astype(F32) * sgc_ref[...].astype(F32))
    merged = jnp.zeros(x.shape, F32)
    for n in range(N_BRANCH):
        gate_n = _sigmoid(_dot(u, wmg_ref[:, n * d:(n + 1) * d]))
        merged = merged + gate_n * _dot(branches[n].astype(BF16), wbr_ref[n])
    y = _dot(merged.astype(BF16), wout_ref[...])
    r = alpha * x + gate_ref[...] * y
    mu = jnp.mean(r, axis=1, keepdims=True)
    rc = r - mu
    var = jnp.mean(rc * rc, axis=1, keepdims=True)
    o_ref[...] = rc * lax.rsqrt(var + LN_EPS) * lng_ref[...] + lnb_ref[...]


def _merge(x, shift, scale, gate, o_a, o_b, o_c, sga, sgb, sgc, wmg_all, wbr_all, wout_all, gna_l, gnb_l,
           lng_l, lnb_l, layer, rows_per_seq, alpha):
    t, d = x.shape
    tm, mod_spec = _row_tiling(t, d, rows_per_seq)
    row_spec = pl.BlockSpec((tm, d), lambda i: (i, 0))
    br_spec = pl.BlockSpec((tm, _BW), lambda i: (i, 0))
    in_specs = [row_spec, mod_spec, mod_spec, mod_spec] + [br_spec] * 6 + [
        _resident((None,) + wmg_all.shape[1:], lambda i: (layer, 0, 0)),
        _resident((None,) + wbr_all.shape[1:], lambda i: (layer, 0, 0, 0)),
        _resident((None,) + wout_all.shape[1:], lambda i: (layer, 0, 0)),
        _resident(gna_l.shape, lambda i: (0, 0)), _resident(gnb_l.shape, lambda i: (0, 0)),
        _resident(lng_l.shape, lambda i: (0, 0)), _resident(lnb_l.shape, lambda i: (0, 0))]
    return pl.pallas_call(
        functools.partial(_merge_kernel, alpha=alpha),
        grid=(t // tm,), in_specs=in_specs, out_specs=row_spec,
        out_shape=jax.ShapeDtypeStruct((t, d), F32),
        compiler_params=_params(("arbitrary",), 56), name="merge_out_projection",
    )(x, shift, scale, gate, o_a, o_b, o_c, sga, sgb, sgc, wmg_all, wbr_all, wout_all, gna_l, gnb_l,
      lng_l, lnb_l)


def _permute_in_weights(w_in):
    widths = (N_HEADS * DK_GLA, N_HEADS * DK_GLA, _BW, _BW, GLA_RANK,
              _BW, _BW, _BW, _BW, _BW, _BW, _BW, N_HEADS, _BW)
    names = ("qa", "ka", "va", "ga", "ra", "qb", "fb", "ib", "gb", "qc", "kc", "vc", "fc", "gc")
    cols, off = {}, 0
    for name, w in zip(names, widths):
        cols[name] = w_in[:, :, off:off + w]
        off += w
    mg = w_in[:, :, off:]
    pad = jnp.zeros(w_in.shape[:2] + (LANES - N_HEADS - GLA_RANK,), w_in.dtype)
    order = [cols[n] for n in ("qa", "ka", "va", "ga", "qb", "fb", "ib", "gb", "qc", "kc", "vc", "gc")]
    order += [cols["fc"], cols["ra"], pad]
    return jnp.concatenate(order, axis=2).astype(BF16), mg.astype(BF16)


def kernel(x_prompt, x_sample, cache_fox_k, cache_fox_v, cache_fox_logf, state_gla, state_hgrn, page_table,
           c_prompt, c_sample, w_in, w_gla_a2, b_gla_a2, norm_gla, hgrn_lb_logits, norm_hgrn, b_fox_f,
           w_branch, w_out, w_ada, b_ada, ln_g, ln_b):
    batch, seq, d = x_prompt.shape
    n_dec = x_sample.shape[0]
    depth = w_in.shape[0]
    n_phys, page = cache_fox_k.shape[1], cache_fox_k.shape[2]
    alpha = (2 * depth) ** 0.25

    w_proj, w_mg = _permute_in_weights(w_in)
    wa2 = jnp.zeros((depth, LANES, N_HEADS * DK_GLA), F32).at[:, N_HEADS:N_HEADS + GLA_RANK, :].set(w_gla_a2)
    wa2 = wa2.astype(BF16)
    ba2 = b_gla_a2.reshape(depth, 1, -1)
    bf_pad = jnp.zeros((depth, 1, LANES), F32).at[:, 0, :N_HEADS].set(b_fox_f)
    w_br = w_branch.astype(BF16)
    w_o = w_out.astype(BF16)
    gna = norm_gla.reshape(depth, 1, -1)
    gnb = norm_hgrn.reshape(depth, 1, -1)
    lng = ln_g.reshape(depth, 1, -1)
    lnb = ln_b.reshape(depth, 1, -1)
    cache_k = cache_fox_k.reshape(depth, n_phys, page * N_HEADS, HEAD_W)
    cache_v = cache_fox_v.reshape(depth, n_phys, page * N_HEADS, HEAD_W)
    cache_lf = jnp.swapaxes(cache_fox_logf, 2, 3)

    lower = _lower_bounds(hgrn_lb_logits).reshape(depth, 1, -1)
    mod = _modulation(jnp.concatenate([c_prompt, c_sample], axis=0), w_ada, b_ada)

    xp = x_prompt.reshape(batch * seq, d)
    xs = x_sample.reshape(n_dec, d)
    outs = {k: [] for k in ("gla_p", "hg_p", "fp", "fs")}
    kp = jnp.zeros((depth, batch * seq * N_HEADS, HEAD_W), F32)
    vp = jnp.zeros_like(kp)
    ks = jnp.zeros((depth, n_dec * N_HEADS, HEAD_W), F32)
    vs = jnp.zeros_like(ks)
    gla_s = jnp.zeros_like(state_gla)
    hg_s = jnp.zeros_like(state_hgrn)
    for l in range(depth):
        mod_p = mod[l, :batch].reshape(batch, 1, 3 * d)
        shift_p, scale_p, gate_p = mod_p[:, :, :d], mod_p[:, :, d:2 * d], mod_p[:, :, 2 * d:]
        mod_s = mod[l, batch:]
        shift_s, scale_s, gate_s = mod_s[:, :d], mod_s[:, d:2 * d], mod_s[:, 2 * d:]

        z = _inproj(xp, shift_p, scale_p, w_proj, wa2[l], ba2[l], lower[l], bf_pad[l], kp, vp, l, seq)
        kp, vp = z["kc"], z["vc"]
        o_a, s_a = _scan_prompt(z["qa"], z["ka"], z["va"], z["ga"], batch, seq, DK_GLA)
        o_b, s_b = _scan_prompt(z["sqb"], z["kb"], z["ib"], z["lf"], batch, seq, HEAD_W)
        cum_t = _forget_cumsum(z["zs"], batch, seq)
        o_c = _fox_attn_prompt(z["qc"], kp, vp, cum_t, l, batch, seq)
        xp = _merge(xp, shift_p, scale_p, gate_p, o_a, o_b, o_c, z["sga"], z["sgb"], z["sgc"],
                    w_mg, w_br, w_o, gna[l], gnb[l], lng[l], lnb[l], l, seq, alpha)
        outs["gla_p"].append(s_a)
        outs["hg_p"].append(s_b)
        outs["fp"].append(z["logf"])

        z = _inproj(xs, shift_s, scale_s, w_proj, wa2[l], ba2[l], lower[l], bf_pad[l], ks, vs, l, 1)
        ks, vs = z["kc"], z["vc"]
        o_a, gla_s = _scan_step(z["qa"], z["ka"], z["va"], z["ga"], state_gla, gla_s, l)
        o_b, hg_s = _scan_step(z["sqb"], z["kb"], z["ib"], z["lf"], state_hgrn, hg_s, l)
        o_c = _fox_decode(l, page_table, z["qc"], ks, vs, z["zs"], cache_k, cache_v, cache_lf)
        xs = _merge(xs, shift_s, scale_s, gate_s, o_a, o_b, o_c, z["sga"], z["sgb"], z["sgc"],
                    w_mg, w_br, w_o, gna[l], gnb[l], lng[l], lnb[l], l, 1, alpha)
        outs["fs"].append(z["logf"])

    return (xp.reshape(batch, seq, d), xs.reshape(n_dec, 1, d),
            jnp.stack(outs["gla_p"]), gla_s, jnp.stack(outs["hg_p"]), hg_s,
            kp.reshape(depth, batch, seq, N_HEADS, HEAD_W), vp.reshape(depth, batch, seq, N_HEADS, HEAD_W),
            jnp.stack(outs["fp"]).reshape(depth, batch, seq, N_HEADS),
            ks.reshape(depth, n_dec, 1, N_HEADS, HEAD_W), vs.reshape(depth, n_dec, 1, N_HEADS, HEAD_W),
            jnp.stack(outs["fs"]).reshape(depth, n_dec, 1, N_HEADS))
```

```python
import functools

import jax
import jax.numpy as jnp
from jax import lax
from jax.experimental import pallas as pl
from jax.experimental.pallas import tpu as pltpu

F32 = jnp.float32
BF16 = jnp.bfloat16

N_HEADS = 4
DK_GLA = 64
HEAD_W = 128
GLA_RANK = 16
GLA_GATE_NORM = 16.0
N_BRANCH = 3
LN_EPS = 1e-5
RMS_EPS = 1e-6
F_FLOOR = 1e-30
MASK_VALUE = -1e30
LOG2E = 1.4426950408889634

LANES = 128
SUBLANES = 8

ROWS_INPROJ = 512
SCAN_CHUNK = 64
SCAN_ROWS = 512
ATTN_BLOCK = 256
ATTN_Q_ROWS = 256
DECODE_ROWS = 8
DECODE_PAGE_GROUPS = 4
DECODE_HOSTS = 2
SCAN_SAFE_EXP = 80.0

_BW = N_HEADS * HEAD_W
_GROUPS = ("qka", "va", "ga", "qb", "fb", "ib", "gb", "qc", "kc", "vc", "gc", "small")
_WIDTHS = dict(qka=_BW, va=_BW, ga=_BW, qb=_BW, fb=_BW, ib=_BW, gb=_BW, qc=_BW, kc=_BW, vc=_BW,
               gc=_BW, small=LANES)
_OFFSETS = {}
_acc = 0
for _g in _GROUPS:
    _OFFSETS[_g] = _acc
    _acc += _WIDTHS[_g]
N_PROJ = _acc


def _dot(a, b):
    return jnp.dot(a, b, preferred_element_type=F32)


def _dot_nt(a, b):
    return lax.dot_general(a, b, (((1,), (1,)), ((), ())), preferred_element_type=F32)


def _dot_tn(a, b):
    return lax.dot_general(a, b, (((0,), (0,)), ((), ())), preferred_element_type=F32)


def _split3(x):
    x1 = x.astype(BF16)
    r1 = x - x1.astype(F32)
    x2 = r1.astype(BF16)
    x3 = (r1 - x2.astype(F32)).astype(BF16)
    return x1, x2, x3


def _dot_sel_lhs(sel, x):
    x1, x2, x3 = _split3(x)
    return (_dot(sel, x1) + _dot(sel, x2)) + _dot(sel, x3)


def _dot_sel_rhs(x, sel):
    x1, x2, x3 = _split3(x)
    return (_dot(x1, sel) + _dot(x2, sel)) + _dot(x3, sel)


def _log_sigmoid(x):
    return jnp.minimum(x, 0.0) - jnp.log1p(jnp.exp(-jnp.abs(x)))


def _sigmoid(x):
    return 1.0 / (1.0 + jnp.exp(-x))


def _silu(x):
    return x * _sigmoid(x)


def _lower_tri(n):
    r = lax.broadcasted_iota(jnp.int32, (n, n), 0)
    c = lax.broadcasted_iota(jnp.int32, (n, n), 1)
    return jnp.where(r >= c, 1.0, 0.0).astype(BF16)


def _params(semantics, vmem_mb):
    return pltpu.CompilerParams(dimension_semantics=semantics, vmem_limit_bytes=vmem_mb * 1024 * 1024)


def _resident(block_shape, index_map):
    return pl.BlockSpec(block_shape, index_map, pipeline_mode=pl.Buffered(1))


def _row_tiling(t, d, rows_per_seq, n_steps=None):
    if rows_per_seq == 1:
        tm = min(ROWS_INPROJ, t)
        return tm, pl.BlockSpec((tm, d), lambda i, pt: (i, 0))
    tm = min(ROWS_INPROJ, rows_per_seq) if n_steps is None else t // n_steps
    assert t % tm == 0 and rows_per_seq % tm == 0 and tm % SUBLANES == 0, (t, rows_per_seq, tm)
    tiles_per_seq = rows_per_seq // tm
    return tm, pl.BlockSpec((None, 1, d), lambda i, pt: (i // tiles_per_seq, 0, 0))


def _lower_bound_kernel(x_ref, o_ref):
    x = x_ref[...]
    depth = x.shape[0]
    e = jnp.exp(x - jnp.max(x, axis=0, keepdims=True))
    p = e / jnp.sum(e, axis=0, keepdims=True)
    acc = jnp.zeros_like(p[0:1])
    for l in range(depth):
        acc = acc + p[l:l + 1]
        o_ref[l:l + 1, :] = jnp.clip(acc - p[0:1], 0.0, 1.0)


def _lower_bounds(logits):
    return pl.pallas_call(
        _lower_bound_kernel, out_shape=jax.ShapeDtypeStruct(logits.shape, F32), name="hgrn_lower_bounds",
    )(logits.astype(F32))


def _mod_kernel(c_ref, w_ref, b_ref, o_ref):
    c = c_ref[...]
    o_ref[...] = _dot(_silu(c).astype(BF16), w_ref[...].astype(BF16)) + b_ref[...]


def _modulation(c_all, w_ada, b_ada):
    depth, d, n3 = w_ada.shape
    rows = c_all.shape[0]
    tn = 512
    return pl.pallas_call(
        _mod_kernel,
        grid=(depth, n3 // tn),
        in_specs=[pl.BlockSpec((rows, d), lambda l, j: (0, 0)),
                  pl.BlockSpec((None, d, tn), lambda l, j: (l, 0, j)),
                  pl.BlockSpec((None, 1, tn), lambda l, j: (l, 0, j))],
        out_specs=pl.BlockSpec((None, rows, tn), lambda l, j: (l, 0, j)),
        out_shape=jax.ShapeDtypeStruct((depth, rows, n3), F32),
        compiler_params=_params(("arbitrary", "arbitrary"), 32),
        name="adaln_modulation",
    )(c_all, w_ada, b_ada.reshape(depth, 1, n3))


N_INPROJ_IN = 10


def _inproj_kernel(pt_ref, *refs, dec):
    ins, outs, guest = _split_host_refs(pt_ref, refs, N_INPROJ_IN, len(_INPROJ_OUTS), dec)
    x_ref, shift_ref, scale_ref, w_ref, wa2_ref, ba2_ref, lb_ref, bf_ref, _, _ = ins
    (qa_ref, ka_ref, va_ref, sga_ref, ga_ref, sqb_ref, lf_ref, kb_ref, ib_ref, sgb_ref,
     qc_ref, kc_ref, vc_ref, sgc_ref, zs_ref, logf_ref) = outs
    if guest is not None:
        guest[0]()
    u = (x_ref[...] * (1.0 + scale_ref[...]) + shift_ref[...]).astype(BF16)
    tm = x_ref.shape[0]

    def proj(name):
        off = _OFFSETS[name]
        return _dot(u, w_ref[:, off:off + _WIDTHS[name]])

    zqk = proj("qka")
    half = N_HEADS * DK_GLA
    qa_ref[...] = (zqk[:, :half] * (DK_GLA ** -0.5)).astype(qa_ref.dtype)
    ka_ref[...] = zqk[:, half:].astype(ka_ref.dtype)
    va_ref[...] = proj("va").astype(va_ref.dtype)
    sga_ref[...] = _silu(proj("ga")).astype(sga_ref.dtype)
    zs = proj("small")
    pre = _dot(zs.astype(BF16), wa2_ref[...]) + ba2_ref[...]
    ga_ref[...] = _log_sigmoid(pre) * (1.0 / GLA_GATE_NORM)
    lane = lax.broadcasted_iota(jnp.int32, zs.shape, 1)
    logf = jnp.where(lane < N_HEADS, _log_sigmoid(zs + bf_ref[...]), 0.0)
    zs_ref[...] = logf
    logf_ref[...] = logf[:, :N_HEADS]
    sqb_ref[...] = _silu(proj("qb")).astype(sqb_ref.dtype)
    fb = proj("fb")
    lb = lb_ref[...]
    f = lb + (1.0 - lb) * _sigmoid(fb)
    lf_ref[...] = jnp.log(jnp.maximum(f, F_FLOOR))
    kb_ref[...] = ((1.0 - lb) * _sigmoid(-fb)).astype(kb_ref.dtype)
    ib_ref[...] = proj("ib").astype(ib_ref.dtype)
    sgb_ref[...] = _silu(proj("gb")).astype(sgb_ref.dtype)
    qc_ref[...] = (proj("qc") * (HEAD_W ** -0.5 * LOG2E)).astype(qc_ref.dtype)
    for name, ref in (("kc", kc_ref), ("vc", vc_ref)):
        z = proj(name)
        for h in range(N_HEADS):
            ref[pl.ds(h, tm, stride=N_HEADS), :] = z[:, h * HEAD_W:(h + 1) * HEAD_W]
    sgc_ref[...] = _silu(proj("gc")).astype(sgc_ref.dtype)
    if guest is not None:
        guest[1]()
        guest[2]()


_INPROJ_OUTS = (
    ("qa", 1, N_HEADS * DK_GLA, BF16), ("ka", 1, N_HEADS * DK_GLA, BF16), ("va", 1, _BW, BF16),
    ("sga", 1, _BW, BF16), ("ga", 1, N_HEADS * DK_GLA, F32),
    ("sqb", 1, _BW, BF16), ("lf", 1, _BW, F32), ("kb", 1, _BW, BF16), ("ib", 1, _BW, BF16),
    ("sgb", 1, _BW, BF16),
    ("qc", 1, _BW, BF16), ("kc", N_HEADS, HEAD_W, F32), ("vc", N_HEADS, HEAD_W, F32), ("sgc", 1, _BW, BF16),
    ("zs", 1, LANES, F32), ("logf", 1, N_HEADS, F32))


def _inproj(page_table, x, shift, scale, w_all, wa2_l, ba2_l, lb_l, bf_l, kc_all, vc_all, layer, rows_per_seq,
            dec=None):
    t, d = x.shape
    n_steps = None if dec is None else dec["qc"].shape[0] // dec["stride"]
    tm, mod_spec = _row_tiling(t, d, rows_per_seq, n_steps)
    names = [name for name, _, _, _ in _INPROJ_OUTS]
    in_specs = [pl.BlockSpec((tm, d), lambda i, pt: (i, 0)), mod_spec, mod_spec,
                _resident((None, d, N_PROJ), lambda i, pt: (layer, 0, 0)),
                _resident(wa2_l.shape, lambda i, pt: (0, 0)),
                _resident(ba2_l.shape, lambda i, pt: (0, 0)),
                _resident(lb_l.shape, lambda i, pt: (0, 0)),
                _resident(bf_l.shape, lambda i, pt: (0, 0)),
                pl.BlockSpec(memory_space=pl.ANY), pl.BlockSpec(memory_space=pl.ANY)]
    args = [x, shift, scale, w_all, wa2_l, ba2_l, lb_l, bf_l, kc_all, vc_all]
    assert len(args) == N_INPROJ_IN
    out_specs, out_shape, scratch = [], [], []
    for name, r, w, dt in _INPROJ_OUTS:
        if name in ("kc", "vc"):
            out_specs.append(pl.BlockSpec((None, tm * r, w), lambda i, pt: (layer, i, 0)))
            out_shape.append(jax.ShapeDtypeStruct(kc_all.shape, dt))
        else:
            out_specs.append(pl.BlockSpec((tm * r, w), lambda i, pt: (i, 0)))
            out_shape.append(jax.ShapeDtypeStruct((t * r, w), dt))
    if dec is not None:
        g_specs, g_args, g_out_spec, g_out_shape, scratch = _guest_plumbing(dec, t // tm)
        in_specs += g_specs
        args += g_args
        out_specs.append(g_out_spec)
        out_shape.append(g_out_shape)
        names.append("dec")
    outs = pl.pallas_call(
        functools.partial(_inproj_kernel, dec=None if dec is None else (dec["layer"], dec["stride"], dec["offset"])),
        grid_spec=pltpu.PrefetchScalarGridSpec(
            num_scalar_prefetch=1, grid=(t // tm,), in_specs=in_specs, out_specs=out_specs,
            scratch_shapes=scratch),
        out_shape=out_shape,
        input_output_aliases={1 + 8: names.index("kc"), 1 + 9: names.index("vc")},
        compiler_params=_params(("arbitrary",), 56), name="in_projection",
    )(page_table, *args)
    return dict(zip(names, outs))


def _scan_kernel(q_ref, k_ref, v_ref, g_ref, o_ref, s_ref, st_scr, kf_scr, bf_scr, *, sub, chunk):
    step = pl.program_id(1)
    n_steps = pl.num_programs(1)
    n_chunks = q_ref.shape[0] // chunk
    n_units = q_ref.shape[1] // LANES
    dk = LANES // sub

    @pl.when(step == 0)
    def _():
        st_scr[...] = jnp.zeros_like(st_scr)

    tri = _lower_tri(chunk)
    row_cc = lax.broadcasted_iota(jnp.int32, (chunk, chunk), 0)
    col_cc = lax.broadcasted_iota(jnp.int32, (chunk, chunk), 1)
    causal = row_cc >= col_cc
    lane = lax.broadcasted_iota(jnp.int32, (chunk, LANES), 1)
    mid = chunk // 2

    cums, worst = [], jnp.zeros((1, 1), F32)
    for c in range(n_chunks):
        b = _dot_sel_lhs(tri, g_ref[c * chunk:(c + 1) * chunk, :])
        cums.append(b)
        worst = jnp.maximum(worst, jnp.max(jnp.abs(b - b[mid - 1:mid, :]), keepdims=True))
    fast = worst[0, 0] <= SCAN_SAFE_EXP

    def direct_scores(q, k_h, b):
        kf_scr[...] = k_h
        bf_scr[...] = b

        def body(s, acc):
            k_row = kf_scr[pl.ds(s, 1), :]
            b_row = bf_scr[pl.ds(s, 1), :]
            w = q * k_row * jnp.exp(jnp.minimum(b - b_row, 0.0))
            return jnp.where(col_cc == s, jnp.sum(w, axis=1, keepdims=True), acc)

        return lax.fori_loop(0, chunk, body, jnp.zeros((chunk, chunk), F32))

    def run(factorised):
        st = [st_scr[h] for h in range(N_HEADS)]
        for c in range(n_chunks):
            rs = slice(c * chunk, (c + 1) * chunk)
            for u in range(n_units):
                sl = slice(u * LANES, (u + 1) * LANES)
                b = cums[c][:, sl]
                b_end = b[chunk - 1:chunk, :]
                a = b - b[mid - 1:mid, :]
                q = q_ref[rs, sl].astype(F32)
                k = k_ref[rs, sl].astype(F32)
                q_in = (q * jnp.exp(b)).astype(BF16)
                k_out = k * jnp.exp(b_end - b)
                decay_end = jnp.exp(b_end)
                if factorised:
                    qs = (q * jnp.exp(a)).astype(BF16)
                    ks = k * jnp.exp(-a)
                for j in range(sub):
                    h = u * sub + j
                    hs = slice(h * HEAD_W, (h + 1) * HEAD_W)
                    in_head = None if sub == 1 else (lane // dk) == j

                    def own(x, in_head=in_head):
                        return x if in_head is None else jnp.where(in_head, x, 0.0)

                    if factorised:
                        scores = _dot_nt(qs, own(ks).astype(BF16))
                    else:
                        scores = direct_scores(q, own(k), b)
                    scores = jnp.where(causal, scores, 0.0)
                    v = v_ref[rs, hs]
                    o = _dot(scores.astype(BF16), v) + _dot_nt(q_in, st[h].astype(BF16))
                    o_ref[rs, hs] = o.astype(o_ref.dtype)
                    st[h] = st[h] * decay_end + _dot_tn(v, own(k_out).astype(BF16))
        for h in range(N_HEADS):
            st_scr[h] = st[h]

    @pl.when(fast)
    def _():
        run(True)

    @pl.when(jnp.logical_not(fast))
    def _():
        run(False)

    @pl.when(step == n_steps - 1)
    def _():
        for h in range(N_HEADS):
            j = h % sub
            s_full = jnp.transpose(st_scr[h])
            s_ref[h] = s_full[j * dk:(j + 1) * dk, :]


def _scan_prompt(q, k, v, g, batch, seq, dk):
    t = q.shape[0]
    sub = LANES // dk
    rows = min(SCAN_ROWS, seq)
    chunk = min(SCAN_CHUNK, rows)
    n_steps = seq // rows
    wq = q.shape[1]

    def at(b, c):
        return (b * n_steps + c, 0)

    o, s = pl.pallas_call(
        functools.partial(_scan_kernel, sub=sub, chunk=chunk),
        grid=(batch, n_steps),
        in_specs=[pl.BlockSpec((rows, wq), at), pl.BlockSpec((rows, wq), at),
                  pl.BlockSpec((rows, _BW), at), pl.BlockSpec((rows, wq), at)],
        out_specs=[pl.BlockSpec((rows, _BW), at),
                   pl.BlockSpec((None, N_HEADS, dk, HEAD_W), lambda b, c: (b, 0, 0, 0))],
        out_shape=[jax.ShapeDtypeStruct((t, _BW), BF16),
                   jax.ShapeDtypeStruct((batch, N_HEADS, dk, HEAD_W), F32)],
        scratch_shapes=[pltpu.VMEM((N_HEADS, HEAD_W, LANES), F32),
                        pltpu.VMEM((chunk, LANES), F32), pltpu.VMEM((chunk, LANES), F32)],
        compiler_params=_params(("arbitrary", "arbitrary"), 32),
        name="linear_scan_prompt",
    )(q, k, v, g)
    return o, s


def _scan_step_kernel(q_ref, k_ref, v_ref, g_ref, s_ref, so_all_ref, o_ref, so_ref):
    del so_all_ref
    rows = q_ref.shape[0]
    dk = s_ref.shape[2]
    for i in range(rows):
        for h in range(N_HEADS):
            ks = slice(h * dk, (h + 1) * dk)
            vs = slice(h * HEAD_W, (h + 1) * HEAD_W)
            decay = jnp.transpose(jnp.exp(g_ref[i:i + 1, ks]))
            k_col = jnp.transpose(k_ref[i:i + 1, ks].astype(F32))
            q_col = jnp.transpose(q_ref[i:i + 1, ks].astype(F32))
            v_row = v_ref[i:i + 1, vs].astype(F32)
            s_new = decay * s_ref[i, h] + k_col * v_row
            so_ref[i, h] = s_new
            o_ref[i:i + 1, vs] = jnp.sum(q_col * s_new, axis=0, keepdims=True).astype(o_ref.dtype)


def _scan_step(q, k, v, g, state_all, new_state_all, layer):
    _, n, _, dk, dv = state_all.shape
    rows = min(DECODE_ROWS, n)
    wq = q.shape[1]
    state_spec = pl.BlockSpec((None, rows, N_HEADS, dk, dv), lambda i: (layer, i, 0, 0, 0))
    return pl.pallas_call(
        _scan_step_kernel,
        grid=(n // rows,),
        in_specs=[pl.BlockSpec((rows, wq), lambda i: (i, 0)), pl.BlockSpec((rows, wq), lambda i: (i, 0)),
                  pl.BlockSpec((rows, _BW), lambda i: (i, 0)), pl.BlockSpec((rows, wq), lambda i: (i, 0)),
                  state_spec, pl.BlockSpec(memory_space=pl.ANY)],
        out_specs=[pl.BlockSpec((rows, _BW), lambda i: (i, 0)), state_spec],
        out_shape=[jax.ShapeDtypeStruct((n, _BW), BF16), jax.ShapeDtypeStruct(state_all.shape, F32)],
        input_output_aliases={5: 1},
        compiler_params=_params(("arbitrary",), 32),
        name="linear_scan_step",
    )(q, k, v, g, state_all, new_state_all)


def _forget_cumsum_kernel(z_ref, o_ref, carry_scr):
    @pl.when(pl.program_id(1) == 0)
    def _():
        carry_scr[...] = jnp.zeros_like(carry_scr)

    x = z_ref[...]
    n = x.shape[0]
    cum = _dot_sel_lhs(_lower_tri(n), x) + carry_scr[...]
    carry_scr[...] = cum[n - 1:n, :]
    o_ref[...] = jnp.transpose(cum * LOG2E)[:SUBLANES, :]


def _forget_cumsum(zs, batch, seq):
    blk = min(ATTN_BLOCK, seq)
    nb = seq // blk
    return pl.pallas_call(
        _forget_cumsum_kernel,
        grid=(batch, nb),
        in_specs=[pl.BlockSpec((blk, LANES), lambda b, c: (b * nb + c, 0))],
        out_specs=pl.BlockSpec((None, None, SUBLANES, blk), lambda b, c: (b, c, 0, 0)),
        out_shape=jax.ShapeDtypeStruct((batch, nb, SUBLANES, blk), F32),
        scratch_shapes=[pltpu.VMEM((1, LANES), F32)],
        compiler_params=_params(("arbitrary", "arbitrary"), 32),
        name="forget_cumsum",
    )(zs)


def _fox_attn_kernel(q_ref, k_ref, v_ref, cum_ref, o_ref, k_scr, v_scr):
    qi = pl.program_id(1)
    tq = q_ref.shape[0]
    tk = cum_ref.shape[2]
    seq = k_scr.shape[1]

    @pl.when(qi == 0)
    def _():
        for h in range(N_HEADS):
            k_scr[h] = k_ref[pl.ds(h, seq, stride=N_HEADS), :].astype(BF16)
            v_scr[h] = v_ref[pl.ds(h, seq, stride=N_HEADS), :].astype(BF16)

    q = [q_ref[:, h * HEAD_W:(h + 1) * HEAD_W] for h in range(N_HEADS)]
    n_full = (qi * tq) // tk

    def block(ki, carry, masked):
        rows = pl.ds(pl.multiple_of(ki * tk, tk), tk)
        out = []
        for h in range(N_HEADS):
            m, l, acc = carry[h]
            s = _dot_nt(q[h], k_scr[h, rows, :]) - cum_ref[ki, h:h + 1, :]
            if masked:
                r = lax.broadcasted_iota(jnp.int32, s.shape, 0) + qi * tq
                c = lax.broadcasted_iota(jnp.int32, s.shape, 1) + ki * tk
                s = jnp.where(r >= c, s, MASK_VALUE)
            m_new = jnp.maximum(m, jnp.max(s, axis=1, keepdims=True))
            p = jnp.exp2(s - m_new)
            alpha = jnp.exp2(m - m_new)
            l = alpha * l + jnp.sum(p, axis=1, keepdims=True)
            acc = alpha * acc + _dot(p.astype(BF16), v_scr[h, rows, :])
            out.append((m_new, l, acc))
        return tuple(out)

    init = tuple((jnp.full((tq, 1), MASK_VALUE, F32), jnp.zeros((tq, 1), F32), jnp.zeros((tq, HEAD_W), F32))
                 for _ in range(N_HEADS))
    carry = lax.fori_loop(0, n_full, lambda ki, c: block(ki, c, False), init)
    final = block(n_full, carry, True)
    for h in range(N_HEADS):
        _, l, acc = final[h]
        o_ref[:, h * HEAD_W:(h + 1) * HEAD_W] = (acc / l).astype(o_ref.dtype)


def _fox_attn_prompt(qc, kc_all, vc_all, cum_t, layer, batch, seq):
    t = qc.shape[0]
    blk = min(ATTN_BLOCK, seq)
    nb = seq // blk
    tq = min(ATTN_Q_ROWS, blk)
    nq = seq // tq
    kv_spec = pl.BlockSpec((None, seq * N_HEADS, HEAD_W), lambda b, i: (layer, b, 0))
    return pl.pallas_call(
        _fox_attn_kernel,
        grid=(batch, nq),
        in_specs=[pl.BlockSpec((tq, _BW), lambda b, i: (b * nq + i, 0)), kv_spec, kv_spec,
                  pl.BlockSpec((None, nb, SUBLANES, blk), lambda b, i: (b, 0, 0, 0))],
        out_specs=pl.BlockSpec((tq, _BW), lambda b, i: (b * nq + i, 0)),
        out_shape=jax.ShapeDtypeStruct((t, _BW), BF16),
        scratch_shapes=[pltpu.VMEM((N_HEADS, seq, HEAD_W), BF16), pltpu.VMEM((N_HEADS, seq, HEAD_W), BF16)],
        compiler_params=_params(("arbitrary", "arbitrary"), 40),
        name="fox_attention_prompt",
    )(qc, kc_all, vc_all, cum_t)


def _decode_guest(pt_ref, q_ref, kn_ref, vn_ref, lfn_ref, ck_hbm, cv_hbm, clf_hbm, o_ref,
                  k_buf, v_buf, lf_buf, sems, *, layer, stride, offset):
    b = pl.program_id(0)
    n_steps = pl.num_programs(0)
    n_pages = k_buf.shape[1]
    slot = b % 2

    def page_copies(step, slot):
        seq = step * stride + offset
        out = []
        for j in range(n_pages):
            pg = pt_ref[seq, j]
            out.append(pltpu.make_async_copy(ck_hbm.at[layer, pg], k_buf.at[slot, j], sems.at[slot, 0]))
            out.append(pltpu.make_async_copy(cv_hbm.at[layer, pg], v_buf.at[slot, j], sems.at[slot, 1]))
            out.append(pltpu.make_async_copy(clf_hbm.at[layer, pg], lf_buf.at[slot, j], sems.at[slot, 2]))
        return out

    def begin():
        @pl.when(b == 0)
        def _():
            for c in page_copies(0, 0):
                c.start()

        for c in page_copies(b, slot):
            c.wait()
        for c in page_copies(jnp.minimum(b + 1, n_steps - 1), 1 - slot):
            c.start()

    def end():
        @pl.when(b == n_steps - 1)
        def _():
            for c in page_copies(b, 1 - slot):
                c.wait()

    return begin, functools.partial(_decode_attend, q_ref, kn_ref, vn_ref, lfn_ref, o_ref,
                                    k_buf.at[slot], v_buf.at[slot], lf_buf.at[slot]), end


def _decode_attend(q_ref, kn_ref, vn_ref, lfn_ref, o_ref, k_buf, v_buf, lf_buf):
    n_pages = k_buf.shape[0]
    k_refs = [k_buf.at[j] for j in range(n_pages)]
    v_refs = [v_buf.at[j] for j in range(n_pages)]
    lf_refs = [lf_buf.at[j] for j in range(n_pages)]
    page = lf_buf.shape[2]
    wide = page * N_HEADS
    rows = n_pages * SUBLANES
    pad = jnp.zeros((SUBLANES - N_HEADS, HEAD_W), F32)

    def head_rows(x):
        return jnp.concatenate([x[:, h * HEAD_W:(h + 1) * HEAD_W] for h in range(N_HEADS)] + [pad], axis=0)

    q4 = head_rows(q_ref[...].astype(F32))
    q4b = q4.astype(BF16)
    kn4 = jnp.concatenate([kn_ref[...], pad], axis=0)
    vn4 = jnp.concatenate([vn_ref[...], pad], axis=0)

    lf_rows = jnp.concatenate([x for j in range(n_pages) for x in (lf_refs[j][...], pad)], axis=0)
    r_p = lax.broadcasted_iota(jnp.int32, (page, page), 0)
    c_p = lax.broadcasted_iota(jnp.int32, (page, page), 1)
    upto = jnp.where(r_p <= c_p, 1.0, 0.0).astype(BF16)
    cum_page = _dot_sel_rhs(lf_rows, upto)
    totals = jnp.broadcast_to(cum_page[:, page - 1:page], (rows, page))
    r_r = lax.broadcasted_iota(jnp.int32, (rows, rows), 0)
    c_r = lax.broadcasted_iota(jnp.int32, (rows, rows), 1)
    earlier = jnp.where(((r_r % SUBLANES) == (c_r % SUBLANES)) & ((c_r // SUBLANES) < (r_r // SUBLANES)),
                        1.0, 0.0).astype(BF16)
    ck = cum_page + _dot_sel_lhs(earlier, totals)
    lfn_col = jnp.transpose(lfn_ref[...])[:SUBLANES, :]
    ck_new = (ck[rows - SUBLANES:, page - 1:page] + lfn_col) * LOG2E
    r_e = lax.broadcasted_iota(jnp.int32, (page, wide), 0)
    c_e = lax.broadcasted_iota(jnp.int32, (page, wide), 1)
    spread = jnp.where((c_e // N_HEADS) == r_e, 1.0, 0.0).astype(BF16)
    ck_wide = _dot_sel_rhs(ck * LOG2E, spread)

    row8 = lax.broadcasted_iota(jnp.int32, (SUBLANES, wide), 0)
    lane = lax.broadcasted_iota(jnp.int32, (SUBLANES, wide), 1)
    own_head = (lane % N_HEADS) == row8

    s_new = jnp.sum(q4 * kn4, axis=1, keepdims=True) - ck_new
    per_group = pl.cdiv(n_pages, DECODE_PAGE_GROUPS)
    parts = [(s_new, jnp.ones_like(s_new), vn4)]
    for g0 in range(0, n_pages, per_group):
        pages = range(g0, min(g0 + per_group, n_pages))
        logits = [jnp.where(own_head,
                            _dot_nt(q4b, k_refs[j][...].astype(BF16)) - ck_wide[j * SUBLANES:(j + 1) * SUBLANES],
                            MASK_VALUE) for j in pages]
        m = functools.reduce(jnp.maximum, [jnp.max(s, axis=1, keepdims=True) for s in logits])
        l = jnp.zeros_like(m)
        acc = jnp.zeros((SUBLANES, HEAD_W), F32)
        for j, s in zip(pages, logits):
            p = jnp.exp2(s - m)
            l = l + jnp.sum(p, axis=1, keepdims=True)
            acc = acc + _dot(p.astype(BF16), v_refs[j][...].astype(BF16))
        parts.append((m, l, acc))
    m_all = functools.reduce(jnp.maximum, [m for m, _, _ in parts])
    l_all = sum(jnp.exp2(m - m_all) * l for m, l, _ in parts)
    out = sum(jnp.exp2(m - m_all) * acc for m, _, acc in parts) / l_all
    o_ref[...] = jnp.concatenate([out[h:h + 1, :] for h in range(N_HEADS)], axis=1).astype(o_ref.dtype)


N_GUEST_IN = 7
N_GUEST_SCRATCH = 4


def _guest_plumbing(dec, n_steps):
    layer, stride, offset = dec["layer"], dec["stride"], dec["offset"]
    n, width = dec["qc"].shape
    depth = dec["kn_all"].shape[0]
    n_pages = dec["n_pages"]
    page = dec["cache_lf"].shape[3]
    assert n == n_steps * stride, (n, n_steps, stride)

    def seq(i):
        return i * stride + offset

    vec_spec = pl.BlockSpec((None, 1, width), lambda i, pt: (seq(i), 0, 0))
    new_spec = pl.BlockSpec((None, None, N_HEADS, HEAD_W), lambda i, pt: (layer, seq(i), 0, 0))
    hbm_spec = pl.BlockSpec(memory_space=pl.ANY)
    in_specs = [vec_spec, new_spec, new_spec, pl.BlockSpec((None, 1, LANES), lambda i, pt: (seq(i), 0, 0)),
                hbm_spec, hbm_spec, hbm_spec]
    args = [dec["qc"].reshape(n, 1, width), dec["kn_all"].reshape(depth, n, N_HEADS, HEAD_W),
            dec["vn_all"].reshape(depth, n, N_HEADS, HEAD_W), dec["lfn"].reshape(n, 1, LANES),
            dec["cache_k"], dec["cache_v"], dec["cache_lf"]]
    out_spec = pl.BlockSpec((None, 1, width), lambda i, pt: (i, 0, 0))
    out_shape = jax.ShapeDtypeStruct((n_steps, 1, width), BF16)
    scratch = [pltpu.VMEM((2, n_pages, page * N_HEADS, HEAD_W), F32),
               pltpu.VMEM((2, n_pages, page * N_HEADS, HEAD_W), F32),
               pltpu.VMEM((2, n_pages, N_HEADS, page), F32),
               pltpu.SemaphoreType.DMA((2, 3))]
    assert len(in_specs) == N_GUEST_IN and len(scratch) == N_GUEST_SCRATCH
    return in_specs, args, out_spec, out_shape, scratch


def _split_host_refs(pt_ref, refs, n_in, n_out, dec):
    if dec is None:
        return refs[:n_in], refs[n_in:n_in + n_out], None
    ins = refs[:n_in]
    g_in = refs[n_in:n_in + N_GUEST_IN]
    outs = refs[n_in + N_GUEST_IN:n_in + N_GUEST_IN + n_out]
    g_out = refs[n_in + N_GUEST_IN + n_out]
    g_scr = refs[n_in + N_GUEST_IN + n_out + 1:]
    layer, stride, offset = dec
    return ins, outs, _decode_guest(pt_ref, *g_in, g_out, *g_scr, layer=layer, stride=stride, offset=offset)


N_MERGE_IN = 17


def _merge_kernel(pt_ref, *refs, alpha, dec):
    ins, (o_ref,), guest = _split_host_refs(pt_ref, refs, N_MERGE_IN, 1, dec)
    (x_ref, shift_ref, scale_ref, gate_ref, oa_ref, ob_ref, oc_ref, sga_ref, sgb_ref, sgc_ref,
     wmg_ref, wbr_ref, wout_ref, gna_ref, gnb_ref, lng_ref, lnb_ref) = ins
    if guest is not None:
        guest[0]()
    x = x_ref[...]
    d = x.shape[1]
    u = (x * (1.0 + scale_ref[...]) + shift_ref[...]).astype(BF16)

    def head_rmsnorm(o, gain):
        parts = []
        for h in range(N_HEADS):
            blk = o[:, h * HEAD_W:(h + 1) * HEAD_W]
            parts.append(blk * lax.rsqrt(jnp.mean(blk * blk, axis=1, keepdims=True) + RMS_EPS))
        return jnp.concatenate(parts, axis=1) * gain

    branches = (
        head_rmsnorm(oa_ref[...].astype(F32), gna_ref[...]) * sga_ref[...].astype(F32),
        head_rmsnorm(ob_ref[...].astype(F32), gnb_ref[...]) * sgb_ref[...].astype(F32),
        oc_ref[...].astype(F32) * sgc_ref[...].astype(F32))
    merged = jnp.zeros(x.shape, F32)
    for n in range(N_BRANCH):
        gate_n = _sigmoid(_dot(u, wmg_ref[:, n * d:(n + 1) * d]))
        merged = merged + gate_n * _dot(branches[n].astype(BF16), wbr_ref[n])
    y = _dot(merged.astype(BF16), wout_ref[...])
    r = alpha * x + gate_ref[...] * y
    mu = jnp.mean(r, axis=1, keepdims=True)
    rc = r - mu
    var = jnp.mean(rc * rc, axis=1, keepdims=True)
    o_ref[...] = rc * lax.rsqrt(var + LN_EPS) * lng_ref[...] + lnb_ref[...]
    if guest is not None:
        guest[1]()
        guest[2]()


def _merge(page_table, x, shift, scale, gate, o_a, o_b, o_c, sga, sgb, sgc, wmg_all, wbr_all, wout_all,
           gna_l, gnb_l, lng_l, lnb_l, layer, rows_per_seq, alpha, dec=None):
    t, d = x.shape
    n_steps = None if dec is None else dec["qc"].shape[0] // dec["stride"]
    tm, mod_spec = _row_tiling(t, d, rows_per_seq, n_steps)
    row_spec = pl.BlockSpec((tm, d), lambda i, pt: (i, 0))
    br_spec = pl.BlockSpec((tm, _BW), lambda i, pt: (i, 0))
    in_specs = [row_spec, mod_spec, mod_spec, mod_spec] + [br_spec] * 6 + [
        _resident((None,) + wmg_all.shape[1:], lambda i, pt: (layer, 0, 0)),
        _resident((None,) + wbr_all.shape[1:], lambda i, pt: (layer, 0, 0, 0)),
        _resident((None,) + wout_all.shape[1:], lambda i, pt: (layer, 0, 0)),
        _resident(gna_l.shape, lambda i, pt: (0, 0)), _resident(gnb_l.shape, lambda i, pt: (0, 0)),
        _resident(lng_l.shape, lambda i, pt: (0, 0)), _resident(lnb_l.shape, lambda i, pt: (0, 0))]
    args = [x, shift, scale, gate, o_a, o_b, o_c, sga, sgb, sgc, wmg_all, wbr_all, wout_all, gna_l, gnb_l,
            lng_l, lnb_l]
    assert len(args) == N_MERGE_IN
    out_specs, out_shape, scratch = [row_spec], [jax.ShapeDtypeStruct((t, d), F32)], []
    if dec is not None:
        g_specs, g_args, g_out_spec, g_out_shape, scratch = _guest_plumbing(dec, t // tm)
        in_specs += g_specs
        args += g_args
        out_specs.append(g_out_spec)
        out_shape.append(g_out_shape)
    outs = pl.pallas_call(
        functools.partial(_merge_kernel, alpha=alpha,
                          dec=None if dec is None else (dec["layer"], dec["stride"], dec["offset"])),
        grid_spec=pltpu.PrefetchScalarGridSpec(
            num_scalar_prefetch=1, grid=(t // tm,), in_specs=in_specs, out_specs=out_specs,
            scratch_shapes=scratch),
        out_shape=out_shape,
        compiler_params=_params(("arbitrary",), 56), name="merge_out_projection",
    )(page_table, *args)
    return outs if dec is not None else outs[0]


def _permute_in_weights(w_in):
    widths = (N_HEADS * DK_GLA, N_HEADS * DK_GLA, _BW, _BW, GLA_RANK,
              _BW, _BW, _BW, _BW, _BW, _BW, _BW, N_HEADS, _BW)
    names = ("qa", "ka", "va", "ga", "ra", "qb", "fb", "ib", "gb", "qc", "kc", "vc", "fc", "gc")
    cols, off = {}, 0
    for name, w in zip(names, widths):
        cols[name] = w_in[:, :, off:off + w]
        off += w
    mg = w_in[:, :, off:]
    pad = jnp.zeros(w_in.shape[:2] + (LANES - N_HEADS - GLA_RANK,), w_in.dtype)
    order = [cols[n] for n in ("qa", "ka", "va", "ga", "qb", "fb", "ib", "gb", "qc", "kc", "vc", "gc")]
    order += [cols["fc"], cols["ra"], pad]
    return jnp.concatenate(order, axis=2).astype(BF16), mg.astype(BF16)


def kernel(x_prompt, x_sample, cache_fox_k, cache_fox_v, cache_fox_logf, state_gla, state_hgrn, page_table,
           c_prompt, c_sample, w_in, w_gla_a2, b_gla_a2, norm_gla, hgrn_lb_logits, norm_hgrn, b_fox_f,
           w_branch, w_out, w_ada, b_ada, ln_g, ln_b):
    batch, seq, d = x_prompt.shape
    n_dec = x_sample.shape[0]
    depth = w_in.shape[0]
    n_phys, page = cache_fox_k.shape[1], cache_fox_k.shape[2]
    alpha = (2 * depth) ** 0.25

    w_proj, w_mg = _permute_in_weights(w_in)
    wa2 = jnp.zeros((depth, LANES, N_HEADS * DK_GLA), F32).at[:, N_HEADS:N_HEADS + GLA_RANK, :].set(w_gla_a2)
    wa2 = wa2.astype(BF16)
    ba2 = b_gla_a2.reshape(depth, 1, -1)
    bf_pad = jnp.zeros((depth, 1, LANES), F32).at[:, 0, :N_HEADS].set(b_fox_f)
    w_br = w_branch.astype(BF16)
    w_o = w_out.astype(BF16)
    gna = norm_gla.reshape(depth, 1, -1)
    gnb = norm_hgrn.reshape(depth, 1, -1)
    lng = ln_g.reshape(depth, 1, -1)
    lnb = ln_b.reshape(depth, 1, -1)
    cache_k = cache_fox_k.reshape(depth, n_phys, page * N_HEADS, HEAD_W)
    cache_v = cache_fox_v.reshape(depth, n_phys, page * N_HEADS, HEAD_W)
    cache_lf = jnp.swapaxes(cache_fox_logf, 2, 3)

    lower = _lower_bounds(hgrn_lb_logits).reshape(depth, 1, -1)
    mod = _modulation(jnp.concatenate([c_prompt, c_sample], axis=0), w_ada, b_ada)

    xp = x_prompt.reshape(batch * seq, d)
    xs = x_sample.reshape(n_dec, d)
    outs = {k: [] for k in ("gla_p", "hg_p", "fp", "fs")}
    kp = jnp.zeros((depth, batch * seq * N_HEADS, HEAD_W), F32)
    vp = jnp.zeros_like(kp)
    ks = jnp.zeros((depth, n_dec * N_HEADS, HEAD_W), F32)
    vs = jnp.zeros_like(ks)
    gla_s = jnp.zeros_like(state_gla)
    hg_s = jnp.zeros_like(state_hgrn)
    for l in range(depth):
        mod_p = mod[l, :batch].reshape(batch, 1, 3 * d)
        shift_p, scale_p, gate_p = mod_p[:, :, :d], mod_p[:, :, d:2 * d], mod_p[:, :, 2 * d:]
        mod_s = mod[l, batch:]
        shift_s, scale_s, gate_s = mod_s[:, :d], mod_s[:, d:2 * d], mod_s[:, 2 * d:]

        zs = _inproj(page_table, xs, shift_s, scale_s, w_proj, wa2[l], ba2[l], lower[l], bf_pad[l], ks, vs, l, 1)
        ks, vs = zs["kc"], zs["vc"]
        o_as, gla_s = _scan_step(zs["qa"], zs["ka"], zs["va"], zs["ga"], state_gla, gla_s, l)
        o_bs, hg_s = _scan_step(zs["sqb"], zs["kb"], zs["ib"], zs["lf"], state_hgrn, hg_s, l)
        dec = dict(layer=l, stride=DECODE_HOSTS, n_pages=page_table.shape[1], qc=zs["qc"], kn_all=ks, vn_all=vs,
                   lfn=zs["zs"], cache_k=cache_k, cache_v=cache_v, cache_lf=cache_lf)

        z = _inproj(page_table, xp, shift_p, scale_p, w_proj, wa2[l], ba2[l], lower[l], bf_pad[l], kp, vp, l, seq,
                    dec=dict(dec, offset=0))
        kp, vp = z["kc"], z["vc"]
        o_a, s_a = _scan_prompt(z["qa"], z["ka"], z["va"], z["ga"], batch, seq, DK_GLA)
        o_b, s_b = _scan_prompt(z["sqb"], z["kb"], z["ib"], z["lf"], batch, seq, HEAD_W)
        cum_t = _forget_cumsum(z["zs"], batch, seq)
        o_c = _fox_attn_prompt(z["qc"], kp, vp, cum_t, l, batch, seq)
        xp, dec_odd = _merge(page_table, xp, shift_p, scale_p, gate_p, o_a, o_b, o_c, z["sga"], z["sgb"], z["sgc"],
                             w_mg, w_br, w_o, gna[l], gnb[l], lng[l], lnb[l], l, seq, alpha,
                             dec=dict(dec, offset=1))
        outs["gla_p"].append(s_a)
        outs["hg_p"].append(s_b)
        outs["fp"].append(z["logf"])

        o_cs = jnp.concatenate([z["dec"], dec_odd], axis=1).reshape(n_dec, N_HEADS * HEAD_W)
        xs = _merge(page_table, xs, shift_s, scale_s, gate_s, o_as, o_bs, o_cs, zs["sga"], zs["sgb"], zs["sgc"],
                    w_mg, w_br, w_o, gna[l], gnb[l], lng[l], lnb[l], l, 1, alpha)
        outs["fs"].append(zs["logf"])

    return (xp.reshape(batch, seq, d), xs.reshape(n_dec, 1, d),
            jnp.stack(outs["gla_p"]), gla_s, jnp.stack(outs["hg_p"]), hg_s,
            kp.reshape(depth, batch, seq, N_HEADS, HEAD_W), vp.reshape(depth, batch, seq, N_HEADS, HEAD_W),
            jnp.stack(outs["fp"]).reshape(depth, batch, seq, N_HEADS),
            ks.reshape(depth, n_dec, 1, N_HEADS, HEAD_W), vs.reshape(depth, n_dec, 1, N_HEADS, HEAD_W),
            jnp.stack(outs["fs"]).reshape(depth, n_dec, 1, N_HEADS))
```

```python
import functools

import jax
import jax.numpy as jnp
from jax import lax
from jax.experimental import pallas as pl
from jax.experimental.pallas import tpu as pltpu

F32 = jnp.float32
BF16 = jnp.bfloat16

N_HEADS = 4
DK_GLA = 64
HEAD_W = 128
GLA_RANK = 16
GLA_GATE_NORM = 16.0
N_BRANCH = 3
LN_EPS = 1e-5
RMS_EPS = 1e-6
F_FLOOR = 1e-30
MASK_VALUE = -1e30
LOG2E = 1.4426950408889634

LANES = 128
SUBLANES = 8

ROWS_INPROJ = 512
SCAN_CHUNK = 64
SCAN_ROWS = 1024
ATTN_BLOCK = 256
ATTN_Q_ROWS = 256
DECODE_ROWS = 8
DECODE_PAGE_GROUPS = 4
DECODE_HOSTS = 2
SCAN_SAFE_EXP = 80.0

_BW = N_HEADS * HEAD_W
_GROUPS = ("qka", "va", "ga", "qb", "fb", "ib", "gb", "qc", "kc", "vc", "gc", "small")
_WIDTHS = dict(qka=_BW, va=_BW, ga=_BW, qb=_BW, fb=_BW, ib=_BW, gb=_BW, qc=_BW, kc=_BW, vc=_BW,
               gc=_BW, small=LANES)
_OFFSETS = {}
_acc = 0
for _g in _GROUPS:
    _OFFSETS[_g] = _acc
    _acc += _WIDTHS[_g]
N_PROJ = _acc


def _dot(a, b):
    return jnp.dot(a, b, preferred_element_type=F32)


def _dot_nt(a, b):
    return lax.dot_general(a, b, (((1,), (1,)), ((), ())), preferred_element_type=F32)


def _dot_tn(a, b):
    return lax.dot_general(a, b, (((0,), (0,)), ((), ())), preferred_element_type=F32)


def _split3(x):
    x1 = x.astype(BF16)
    r1 = x - x1.astype(F32)
    x2 = r1.astype(BF16)
    x3 = (r1 - x2.astype(F32)).astype(BF16)
    return x1, x2, x3


def _dot_sel_lhs(sel, x):
    x1, x2, x3 = _split3(x)
    return (_dot(sel, x1) + _dot(sel, x2)) + _dot(sel, x3)


def _dot_sel_rhs(x, sel):
    x1, x2, x3 = _split3(x)
    return (_dot(x1, sel) + _dot(x2, sel)) + _dot(x3, sel)


def _log_sigmoid(x):
    return jnp.minimum(x, 0.0) - jnp.log1p(jnp.exp(-jnp.abs(x)))


def _sigmoid(x):
    return 1.0 / (1.0 + jnp.exp(-x))


def _silu(x):
    return x * _sigmoid(x)


def _lower_tri(n):
    r = lax.broadcasted_iota(jnp.int32, (n, n), 0)
    c = lax.broadcasted_iota(jnp.int32, (n, n), 1)
    return jnp.where(r >= c, 1.0, 0.0).astype(BF16)


def _params(semantics, vmem_mb):
    return pltpu.CompilerParams(dimension_semantics=semantics, vmem_limit_bytes=vmem_mb * 1024 * 1024)


def _resident(block_shape, index_map):
    return pl.BlockSpec(block_shape, index_map, pipeline_mode=pl.Buffered(1))


def _row_tiling(t, d, rows_per_seq, n_steps=None):
    if rows_per_seq == 1:
        tm = min(ROWS_INPROJ, t)
        return tm, pl.BlockSpec((tm, d), lambda i, pt: (i, 0))
    tm = min(ROWS_INPROJ, rows_per_seq) if n_steps is None else t // n_steps
    assert t % tm == 0 and rows_per_seq % tm == 0 and tm % SUBLANES == 0, (t, rows_per_seq, tm)
    tiles_per_seq = rows_per_seq // tm
    return tm, pl.BlockSpec((None, 1, d), lambda i, pt: (i // tiles_per_seq, 0, 0))


def _lower_bound_kernel(x_ref, o_ref):
    x = x_ref[...]
    depth = x.shape[0]
    e = jnp.exp(x - jnp.max(x, axis=0, keepdims=True))
    p = e / jnp.sum(e, axis=0, keepdims=True)
    acc = jnp.zeros_like(p[0:1])
    for l in range(depth):
        acc = acc + p[l:l + 1]
        o_ref[l:l + 1, :] = jnp.clip(acc - p[0:1], 0.0, 1.0)


def _lower_bounds(logits):
    return pl.pallas_call(
        _lower_bound_kernel, out_shape=jax.ShapeDtypeStruct(logits.shape, F32), name="hgrn_lower_bounds",
    )(logits.astype(F32))


def _mod_kernel(c_ref, w_ref, b_ref, o_ref):
    c = c_ref[...]
    o_ref[...] = _dot(_silu(c).astype(BF16), w_ref[...].astype(BF16)) + b_ref[...]


def _modulation(c_all, w_ada, b_ada):
    depth, d, n3 = w_ada.shape
    rows = c_all.shape[0]
    tn = 512
    return pl.pallas_call(
        _mod_kernel,
        grid=(depth, n3 // tn),
        in_specs=[pl.BlockSpec((rows, d), lambda l, j: (0, 0)),
                  pl.BlockSpec((None, d, tn), lambda l, j: (l, 0, j)),
                  pl.BlockSpec((None, 1, tn), lambda l, j: (l, 0, j))],
        out_specs=pl.BlockSpec((None, rows, tn), lambda l, j: (l, 0, j)),
        out_shape=jax.ShapeDtypeStruct((depth, rows, n3), F32),
        compiler_params=_params(("arbitrary", "arbitrary"), 32),
        name="adaln_modulation",
    )(c_all, w_ada, b_ada.reshape(depth, 1, n3))


def _run_with_guest(segments, guest):
    if guest is None:
        for seg in segments:
            seg()
        return
    _, stages, end = guest
    done = 0
    for i, seg in enumerate(segments):
        seg()
        while done < len(stages) and (done + 1) * len(segments) <= (i + 1) * len(stages):
            stages[done]()
            done += 1
    for stage in stages[done:]:
        stage()
    end()


N_INPROJ_IN = 10


def _inproj_kernel(pt_ref, *refs, dec):
    ins, outs, guest = _split_host_refs(pt_ref, refs, N_INPROJ_IN, len(_INPROJ_OUTS), dec)
    x_ref, shift_ref, scale_ref, w_ref, wa2_ref, ba2_ref, lb_ref, bf_ref, _, _ = ins
    (qa_ref, ka_ref, va_ref, sga_ref, ga_ref, sqb_ref, lf_ref, kb_ref, ib_ref, sgb_ref,
     qc_ref, kc_ref, vc_ref, sgc_ref, zs_ref, logf_ref) = outs
    if guest is not None:
        guest[0]()
    u = (x_ref[...] * (1.0 + scale_ref[...]) + shift_ref[...]).astype(BF16)
    tm = x_ref.shape[0]

    def proj(name):
        off = _OFFSETS[name]
        return _dot(u, w_ref[:, off:off + _WIDTHS[name]])

    def plain(name, ref, fn=None):
        def seg():
            z = proj(name)
            ref[...] = (z if fn is None else fn(z)).astype(ref.dtype)
        return seg

    def gla_qk():
        zqk = proj("qka")
        half = N_HEADS * DK_GLA
        qa_ref[...] = (zqk[:, :half] * (DK_GLA ** -0.5)).astype(qa_ref.dtype)
        ka_ref[...] = zqk[:, half:].astype(ka_ref.dtype)

    def small():
        zs = proj("small")
        pre = _dot(zs.astype(BF16), wa2_ref[...]) + ba2_ref[...]
        ga_ref[...] = _log_sigmoid(pre) * (1.0 / GLA_GATE_NORM)
        lane = lax.broadcasted_iota(jnp.int32, zs.shape, 1)
        logf = jnp.where(lane < N_HEADS, _log_sigmoid(zs + bf_ref[...]), 0.0)
        zs_ref[...] = logf
        logf_ref[...] = logf[:, :N_HEADS]

    def hgrn_forget():
        fb = proj("fb")
        lb = lb_ref[...]
        f = lb + (1.0 - lb) * _sigmoid(fb)
        lf_ref[...] = jnp.log(jnp.maximum(f, F_FLOOR))
        kb_ref[...] = ((1.0 - lb) * _sigmoid(-fb)).astype(kb_ref.dtype)

    def head_rows(name, ref):
        def seg():
            z = proj(name)
            for h in range(N_HEADS):
                ref[pl.ds(h, tm, stride=N_HEADS), :] = z[:, h * HEAD_W:(h + 1) * HEAD_W]
        return seg

    segments = [gla_qk, plain("va", va_ref), plain("ga", sga_ref, _silu), small,
                plain("qb", sqb_ref, _silu), hgrn_forget, plain("ib", ib_ref), plain("gb", sgb_ref, _silu),
                plain("qc", qc_ref, lambda z: z * (HEAD_W ** -0.5 * LOG2E)),
                head_rows("kc", kc_ref), head_rows("vc", vc_ref), plain("gc", sgc_ref, _silu)]
    _run_with_guest(segments, guest)


_INPROJ_OUTS = (
    ("qa", 1, N_HEADS * DK_GLA, BF16), ("ka", 1, N_HEADS * DK_GLA, BF16), ("va", 1, _BW, BF16),
    ("sga", 1, _BW, BF16), ("ga", 1, N_HEADS * DK_GLA, F32),
    ("sqb", 1, _BW, BF16), ("lf", 1, _BW, F32), ("kb", 1, _BW, BF16), ("ib", 1, _BW, BF16),
    ("sgb", 1, _BW, BF16),
    ("qc", 1, _BW, BF16), ("kc", N_HEADS, HEAD_W, F32), ("vc", N_HEADS, HEAD_W, F32), ("sgc", 1, _BW, BF16),
    ("zs", 1, LANES, F32), ("logf", 1, N_HEADS, F32))


def _inproj(page_table, x, shift, scale, w_all, wa2_l, ba2_l, lb_l, bf_l, kc_all, vc_all, layer, rows_per_seq,
            dec=None):
    t, d = x.shape
    n_steps = None if dec is None else dec["qc"].shape[0] // dec["stride"]
    tm, mod_spec = _row_tiling(t, d, rows_per_seq, n_steps)
    names = [name for name, _, _, _ in _INPROJ_OUTS]
    in_specs = [pl.BlockSpec((tm, d), lambda i, pt: (i, 0)), mod_spec, mod_spec,
                _resident((None, d, N_PROJ), lambda i, pt: (layer, 0, 0)),
                _resident(wa2_l.shape, lambda i, pt: (0, 0)),
                _resident(ba2_l.shape, lambda i, pt: (0, 0)),
                _resident(lb_l.shape, lambda i, pt: (0, 0)),
                _resident(bf_l.shape, lambda i, pt: (0, 0)),
                pl.BlockSpec(memory_space=pl.ANY), pl.BlockSpec(memory_space=pl.ANY)]
    args = [x, shift, scale, w_all, wa2_l, ba2_l, lb_l, bf_l, kc_all, vc_all]
    assert len(args) == N_INPROJ_IN
    out_specs, out_shape, scratch = [], [], []
    for name, r, w, dt in _INPROJ_OUTS:
        if name in ("kc", "vc"):
            out_specs.append(pl.BlockSpec((None, tm * r, w), lambda i, pt: (layer, i, 0)))
            out_shape.append(jax.ShapeDtypeStruct(kc_all.shape, dt))
        else:
            out_specs.append(pl.BlockSpec((tm * r, w), lambda i, pt: (i, 0)))
            out_shape.append(jax.ShapeDtypeStruct((t * r, w), dt))
    if dec is not None:
        g_specs, g_args, g_out_spec, g_out_shape, scratch = _guest_plumbing(dec, t // tm)
        in_specs += g_specs
        args += g_args
        out_specs.append(g_out_spec)
        out_shape.append(g_out_shape)
        names.append("dec")
    outs = pl.pallas_call(
        functools.partial(_inproj_kernel, dec=None if dec is None else (dec["layer"], dec["stride"], dec["offset"])),
        grid_spec=pltpu.PrefetchScalarGridSpec(
            num_scalar_prefetch=1, grid=(t // tm,), in_specs=in_specs, out_specs=out_specs,
            scratch_shapes=scratch),
        out_shape=out_shape,
        input_output_aliases={1 + 8: names.index("kc"), 1 + 9: names.index("vc")},
        compiler_params=_params(("arbitrary",), 56), name="in_projection",
    )(page_table, *args)
    return dict(zip(names, outs))


def _scan_kernel(q_ref, k_ref, v_ref, g_ref, o_ref, s_ref, st_scr, kf_scr, bf_scr, *, sub, chunk):
    step = pl.program_id(1)
    n_steps = pl.num_programs(1)
    n_chunks = q_ref.shape[0] // chunk
    n_units = q_ref.shape[1] // LANES
    dk = LANES // sub

    @pl.when(step == 0)
    def _():
        st_scr[...] = jnp.zeros_like(st_scr)

    tri = _lower_tri(chunk)
    row_cc = lax.broadcasted_iota(jnp.int32, (chunk, chunk), 0)
    col_cc = lax.broadcasted_iota(jnp.int32, (chunk, chunk), 1)
    causal = row_cc >= col_cc
    lane = lax.broadcasted_iota(jnp.int32, (chunk, LANES), 1)
    mid = chunk // 2

    cums, worst = [], jnp.zeros((1, 1), F32)
    for c in range(n_chunks):
        b = _dot_sel_lhs(tri, g_ref[c * chunk:(c + 1) * chunk, :])
        cums.append(b)
        worst = jnp.maximum(worst, jnp.max(jnp.abs(b - b[mid - 1:mid, :]), keepdims=True))
    fast = worst[0, 0] <= SCAN_SAFE_EXP

    def direct_scores(q, k_h, b):
        kf_scr[...] = k_h
        bf_scr[...] = b

        def body(s, acc):
            k_row = kf_scr[pl.ds(s, 1), :]
            b_row = bf_scr[pl.ds(s, 1), :]
            w = q * k_row * jnp.exp(jnp.minimum(b - b_row, 0.0))
            return jnp.where(col_cc == s, jnp.sum(w, axis=1, keepdims=True), acc)

        return lax.fori_loop(0, chunk, body, jnp.zeros((chunk, chunk), F32))

    def run(factorised):
        st = [st_scr[h] for h in range(N_HEADS)]
        for c in range(n_chunks):
            rs = slice(c * chunk, (c + 1) * chunk)
            for u in range(n_units):
                sl = slice(u * LANES, (u + 1) * LANES)
                b = cums[c][:, sl]
                b_end = b[chunk - 1:chunk, :]
                a = b - b[mid - 1:mid, :]
                q = q_ref[rs, sl].astype(F32)
                k = k_ref[rs, sl].astype(F32)
                q_in = (q * jnp.exp(b)).astype(BF16)
                k_out = k * jnp.exp(b_end - b)
                decay_end = jnp.exp(b_end)
                if factorised:
                    qs = (q * jnp.exp(a)).astype(BF16)
                    ks = k * jnp.exp(-a)
                for j in range(sub):
                    h = u * sub + j
                    hs = slice(h * HEAD_W, (h + 1) * HEAD_W)
                    in_head = None if sub == 1 else (lane // dk) == j

                    def own(x, in_head=in_head):
                        return x if in_head is None else jnp.where(in_head, x, 0.0)

                    if factorised:
                        scores = _dot_nt(qs, own(ks).astype(BF16))
                    else:
                        scores = direct_scores(q, own(k), b)
                    scores = jnp.where(causal, scores, 0.0)
                    v = v_ref[rs, hs]
                    o = _dot(scores.astype(BF16), v) + _dot_nt(q_in, st[h].astype(BF16))
                    o_ref[rs, hs] = o.astype(o_ref.dtype)
                    st[h] = st[h] * decay_end + _dot_tn(v, own(k_out).astype(BF16))
        for h in range(N_HEADS):
            st_scr[h] = st[h]

    @pl.when(fast)
    def _():
        run(True)

    @pl.when(jnp.logical_not(fast))
    def _():
        run(False)

    @pl.when(step == n_steps - 1)
    def _():
        for h in range(N_HEADS):
            j = h % sub
            s_full = jnp.transpose(st_scr[h])
            s_ref[h] = s_full[j * dk:(j + 1) * dk, :]


def _scan_prompt(q, k, v, g, batch, seq, dk):
    t = q.shape[0]
    sub = LANES // dk
    rows = min(SCAN_ROWS, seq)
    chunk = min(SCAN_CHUNK, rows)
    n_steps = seq // rows
    wq = q.shape[1]

    def at(b, c):
        return (b * n_steps + c, 0)

    o, s = pl.pallas_call(
        functools.partial(_scan_kernel, sub=sub, chunk=chunk),
        grid=(batch, n_steps),
        in_specs=[pl.BlockSpec((rows, wq), at), pl.BlockSpec((rows, wq), at),
                  pl.BlockSpec((rows, _BW), at), pl.BlockSpec((rows, wq), at)],
        out_specs=[pl.BlockSpec((rows, _BW), at),
                   pl.BlockSpec((None, N_HEADS, dk, HEAD_W), lambda b, c: (b, 0, 0, 0))],
        out_shape=[jax.ShapeDtypeStruct((t, _BW), BF16),
                   jax.ShapeDtypeStruct((batch, N_HEADS, dk, HEAD_W), F32)],
        scratch_shapes=[pltpu.VMEM((N_HEADS, HEAD_W, LANES), F32),
                        pltpu.VMEM((chunk, LANES), F32), pltpu.VMEM((chunk, LANES), F32)],
        compiler_params=_params(("arbitrary", "arbitrary"), 32),
        name="linear_scan_prompt",
    )(q, k, v, g)
    return o, s


def _scan_step_kernel(q_ref, k_ref, v_ref, g_ref, s_ref, so_all_ref, o_ref, so_ref):
    del so_all_ref
    rows = q_ref.shape[0]
    dk = s_ref.shape[2]
    for i in range(rows):
        for h in range(N_HEADS):
            ks = slice(h * dk, (h + 1) * dk)
            vs = slice(h * HEAD_W, (h + 1) * HEAD_W)
            decay = jnp.transpose(jnp.exp(g_ref[i:i + 1, ks]))
            k_col = jnp.transpose(k_ref[i:i + 1, ks].astype(F32))
            q_col = jnp.transpose(q_ref[i:i + 1, ks].astype(F32))
            v_row = v_ref[i:i + 1, vs].astype(F32)
            s_new = decay * s_ref[i, h] + k_col * v_row
            so_ref[i, h] = s_new
            o_ref[i:i + 1, vs] = jnp.sum(q_col * s_new, axis=0, keepdims=True).astype(o_ref.dtype)


def _scan_step(q, k, v, g, state_all, new_state_all, layer):
    _, n, _, dk, dv = state_all.shape
    rows = min(DECODE_ROWS, n)
    wq = q.shape[1]
    state_spec = pl.BlockSpec((None, rows, N_HEADS, dk, dv), lambda i: (layer, i, 0, 0, 0))
    return pl.pallas_call(
        _scan_step_kernel,
        grid=(n // rows,),
        in_specs=[pl.BlockSpec((rows, wq), lambda i: (i, 0)), pl.BlockSpec((rows, wq), lambda i: (i, 0)),
                  pl.BlockSpec((rows, _BW), lambda i: (i, 0)), pl.BlockSpec((rows, wq), lambda i: (i, 0)),
                  state_spec, pl.BlockSpec(memory_space=pl.ANY)],
        out_specs=[pl.BlockSpec((rows, _BW), lambda i: (i, 0)), state_spec],
        out_shape=[jax.ShapeDtypeStruct((n, _BW), BF16), jax.ShapeDtypeStruct(state_all.shape, F32)],
        input_output_aliases={5: 1},
        compiler_params=_params(("arbitrary",), 32),
        name="linear_scan_step",
    )(q, k, v, g, state_all, new_state_all)


def _forget_cumsum_kernel(z_ref, o_ref, carry_scr):
    @pl.when(pl.program_id(1) == 0)
    def _():
        carry_scr[...] = jnp.zeros_like(carry_scr)

    x = z_ref[...]
    n = x.shape[0]
    cum = _dot_sel_lhs(_lower_tri(n), x) + carry_scr[...]
    carry_scr[...] = cum[n - 1:n, :]
    o_ref[...] = jnp.transpose(cum * LOG2E)[:SUBLANES, :]


def _forget_cumsum(zs, batch, seq):
    blk = min(ATTN_BLOCK, seq)
    nb = seq // blk
    return pl.pallas_call(
        _forget_cumsum_kernel,
        grid=(batch, nb),
        in_specs=[pl.BlockSpec((blk, LANES), lambda b, c: (b * nb + c, 0))],
        out_specs=pl.BlockSpec((None, None, SUBLANES, blk), lambda b, c: (b, c, 0, 0)),
        out_shape=jax.ShapeDtypeStruct((batch, nb, SUBLANES, blk), F32),
        scratch_shapes=[pltpu.VMEM((1, LANES), F32)],
        compiler_params=_params(("arbitrary", "arbitrary"), 32),
        name="forget_cumsum",
    )(zs)


def _fox_attn_kernel(q_ref, k_ref, v_ref, cum_ref, o_ref, k_scr, v_scr):
    qi = pl.program_id(1)
    tq = q_ref.shape[0]
    tk = cum_ref.shape[2]
    seq = k_scr.shape[1]

    @pl.when(qi == 0)
    def _():
        for h in range(N_HEADS):
            k_scr[h] = k_ref[pl.ds(h, seq, stride=N_HEADS), :].astype(BF16)
            v_scr[h] = v_ref[pl.ds(h, seq, stride=N_HEADS), :].astype(BF16)

    q = [q_ref[:, h * HEAD_W:(h + 1) * HEAD_W] for h in range(N_HEADS)]
    n_full = (qi * tq) // tk

    def block(ki, carry, masked):
        rows = pl.ds(pl.multiple_of(ki * tk, tk), tk)
        out = []
        for h in range(N_HEADS):
            m, l, acc = carry[h]
            s = _dot_nt(q[h], k_scr[h, rows, :]) - cum_ref[ki, h:h + 1, :]
            if masked:
                r = lax.broadcasted_iota(jnp.int32, s.shape, 0) + qi * tq
                c = lax.broadcasted_iota(jnp.int32, s.shape, 1) + ki * tk
                s = jnp.where(r >= c, s, MASK_VALUE)
            m_new = jnp.maximum(m, jnp.max(s, axis=1, keepdims=True))
            p = jnp.exp2(s - m_new)
            alpha = jnp.exp2(m - m_new)
            l = alpha * l + jnp.sum(p, axis=1, keepdims=True)
            acc = alpha * acc + _dot(p.astype(BF16), v_scr[h, rows, :])
            out.append((m_new, l, acc))
        return tuple(out)

    init = tuple((jnp.full((tq, 1), MASK_VALUE, F32), jnp.zeros((tq, 1), F32), jnp.zeros((tq, HEAD_W), F32))
                 for _ in range(N_HEADS))
    carry = lax.fori_loop(0, n_full, lambda ki, c: block(ki, c, False), init)
    final = block(n_full, carry, True)
    for h in range(N_HEADS):
        _, l, acc = final[h]
        o_ref[:, h * HEAD_W:(h + 1) * HEAD_W] = (acc / l).astype(o_ref.dtype)


def _fox_attn_prompt(qc, kc_all, vc_all, cum_t, layer, batch, seq):
    t = qc.shape[0]
    blk = min(ATTN_BLOCK, seq)
    nb = seq // blk
    tq = min(ATTN_Q_ROWS, blk)
    nq = seq // tq
    kv_spec = pl.BlockSpec((None, seq * N_HEADS, HEAD_W), lambda b, i: (layer, b, 0))
    return pl.pallas_call(
        _fox_attn_kernel,
        grid=(batch, nq),
        in_specs=[pl.BlockSpec((tq, _BW), lambda b, i: (b * nq + i, 0)), kv_spec, kv_spec,
                  pl.BlockSpec((None, nb, SUBLANES, blk), lambda b, i: (b, 0, 0, 0))],
        out_specs=pl.BlockSpec((tq, _BW), lambda b, i: (b * nq + i, 0)),
        out_shape=jax.ShapeDtypeStruct((t, _BW), BF16),
        scratch_shapes=[pltpu.VMEM((N_HEADS, seq, HEAD_W), BF16), pltpu.VMEM((N_HEADS, seq, HEAD_W), BF16)],
        compiler_params=_params(("arbitrary", "arbitrary"), 40),
        name="fox_attention_prompt",
    )(qc, kc_all, vc_all, cum_t)


def _decode_guest(pt_ref, q_ref, kn_ref, vn_ref, lfn_ref, ck_hbm, cv_hbm, clf_hbm, o_ref,
                  k_buf, v_buf, lf_buf, sems, *, layer, stride, offset):
    b = pl.program_id(0)
    n_steps = pl.num_programs(0)
    n_pages = k_buf.shape[1]
    slot = b % 2

    def page_copies(step, slot):
        seq = step * stride + offset
        out = []
        for j in range(n_pages):
            pg = pt_ref[seq, j]
            out.append(pltpu.make_async_copy(ck_hbm.at[layer, pg], k_buf.at[slot, j], sems.at[slot, 0]))
            out.append(pltpu.make_async_copy(cv_hbm.at[layer, pg], v_buf.at[slot, j], sems.at[slot, 1]))
            out.append(pltpu.make_async_copy(clf_hbm.at[layer, pg], lf_buf.at[slot, j], sems.at[slot, 2]))
        return out

    def begin():
        @pl.when(b == 0)
        def _():
            for c in page_copies(0, 0):
                c.start()

        for c in page_copies(b, slot):
            c.wait()
        for c in page_copies(jnp.minimum(b + 1, n_steps - 1), 1 - slot):
            c.start()

    def end():
        @pl.when(b == n_steps - 1)
        def _():
            for c in page_copies(b, 1 - slot):
                c.wait()

    return begin, _decode_attend_stages(q_ref, kn_ref, vn_ref, lfn_ref, o_ref,
                                        k_buf.at[slot], v_buf.at[slot], lf_buf.at[slot]), end


def _decode_attend_stages(q_ref, kn_ref, vn_ref, lfn_ref, o_ref, k_buf, v_buf, lf_buf):
    n_pages = k_buf.shape[0]
    st = {}
    k_refs = [k_buf.at[j] for j in range(n_pages)]
    v_refs = [v_buf.at[j] for j in range(n_pages)]
    lf_refs = [lf_buf.at[j] for j in range(n_pages)]
    page = lf_buf.shape[2]
    wide = page * N_HEADS
    rows = n_pages * SUBLANES
    pad = jnp.zeros((SUBLANES - N_HEADS, HEAD_W), F32)

    def head_rows(x):
        return jnp.concatenate([x[:, h * HEAD_W:(h + 1) * HEAD_W] for h in range(N_HEADS)] + [pad], axis=0)

    def within_pages():
        lf_rows = jnp.concatenate([x for j in range(n_pages) for x in (lf_refs[j][...], pad)], axis=0)
        r_p = lax.broadcasted_iota(jnp.int32, (page, page), 0)
        c_p = lax.broadcasted_iota(jnp.int32, (page, page), 1)
        upto = jnp.where(r_p <= c_p, 1.0, 0.0).astype(BF16)
        st["cum_page"] = _dot_sel_rhs(lf_rows, upto)

    def across_pages():
        cum_page = st["cum_page"]
        totals = jnp.broadcast_to(cum_page[:, page - 1:page], (rows, page))
        r_r = lax.broadcasted_iota(jnp.int32, (rows, rows), 0)
        c_r = lax.broadcasted_iota(jnp.int32, (rows, rows), 1)
        earlier = jnp.where(((r_r % SUBLANES) == (c_r % SUBLANES)) & ((c_r // SUBLANES) < (r_r // SUBLANES)),
                            1.0, 0.0).astype(BF16)
        st["ck"] = cum_page + _dot_sel_lhs(earlier, totals)

    def widen():
        ck = st["ck"]
        lfn_col = jnp.transpose(lfn_ref[...])[:SUBLANES, :]
        ck_new = (ck[rows - SUBLANES:, page - 1:page] + lfn_col) * LOG2E
        r_e = lax.broadcasted_iota(jnp.int32, (page, wide), 0)
        c_e = lax.broadcasted_iota(jnp.int32, (page, wide), 1)
        spread = jnp.where((c_e // N_HEADS) == r_e, 1.0, 0.0).astype(BF16)
        st["ck_wide"] = _dot_sel_rhs(ck * LOG2E, spread)
        q4 = head_rows(q_ref[...].astype(F32))
        st["q4b"] = q4.astype(BF16)
        kn4 = jnp.concatenate([kn_ref[...], pad], axis=0)
        s_new = jnp.sum(q4 * kn4, axis=1, keepdims=True) - ck_new
        st["parts"] = [(s_new, jnp.ones_like(s_new), jnp.concatenate([vn_ref[...], pad], axis=0))]

    def logits_of(pages):
        def stage():
            row8 = lax.broadcasted_iota(jnp.int32, (SUBLANES, wide), 0)
            lane = lax.broadcasted_iota(jnp.int32, (SUBLANES, wide), 1)
            own_head = (lane % N_HEADS) == row8
            ck_wide, q4b = st["ck_wide"], st["q4b"]
            st["logits", pages[0]] = [jnp.where(
                own_head,
                _dot_nt(q4b, k_refs[j][...].astype(BF16)) - ck_wide[j * SUBLANES:(j + 1) * SUBLANES],
                MASK_VALUE) for j in pages]
        return stage

    def values_of(pages):
        def stage():
            logits = st.pop(("logits", pages[0]))
            m = functools.reduce(jnp.maximum, [jnp.max(s, axis=1, keepdims=True) for s in logits])
            l = jnp.zeros_like(m)
            acc = jnp.zeros((SUBLANES, HEAD_W), F32)
            for j, s in zip(pages, logits):
                p = jnp.exp2(s - m)
                l = l + jnp.sum(p, axis=1, keepdims=True)
                acc = acc + _dot(p.astype(BF16), v_refs[j][...].astype(BF16))
            st["parts"].append((m, l, acc))
        return stage

    def combine():
        parts = st["parts"]
        m_all = functools.reduce(jnp.maximum, [m for m, _, _ in parts])
        l_all = sum(jnp.exp2(m - m_all) * l for m, l, _ in parts)
        out = sum(jnp.exp2(m - m_all) * acc for m, _, acc in parts) / l_all
        o_ref[...] = jnp.concatenate([out[h:h + 1, :] for h in range(N_HEADS)], axis=1).astype(o_ref.dtype)

    per_group = pl.cdiv(n_pages, DECODE_PAGE_GROUPS)
    groups = [range(g0, min(g0 + per_group, n_pages)) for g0 in range(0, n_pages, per_group)]
    stages = [within_pages, across_pages, widen, logits_of(groups[0])]
    for prev, nxt in zip(groups[:-1], groups[1:]):
        stages += [logits_of(nxt), values_of(prev)]
    return stages + [values_of(groups[-1]), combine]


N_GUEST_IN = 7
N_GUEST_SCRATCH = 4


def _guest_plumbing(dec, n_steps):
    layer, stride, offset = dec["layer"], dec["stride"], dec["offset"]
    n, width = dec["qc"].shape
    depth = dec["kn_all"].shape[0]
    n_pages = dec["n_pages"]
    page = dec["cache_lf"].shape[3]
    assert n == n_steps * stride, (n, n_steps, stride)

    def seq(i):
        return i * stride + offset

    vec_spec = pl.BlockSpec((None, 1, width), lambda i, pt: (seq(i), 0, 0))
    new_spec = pl.BlockSpec((None, None, N_HEADS, HEAD_W), lambda i, pt: (layer, seq(i), 0, 0))
    hbm_spec = pl.BlockSpec(memory_space=pl.ANY)
    in_specs = [vec_spec, new_spec, new_spec, pl.BlockSpec((None, 1, LANES), lambda i, pt: (seq(i), 0, 0)),
                hbm_spec, hbm_spec, hbm_spec]
    args = [dec["qc"].reshape(n, 1, width), dec["kn_all"].reshape(depth, n, N_HEADS, HEAD_W),
            dec["vn_all"].reshape(depth, n, N_HEADS, HEAD_W), dec["lfn"].reshape(n, 1, LANES),
            dec["cache_k"], dec["cache_v"], dec["cache_lf"]]
    out_spec = pl.BlockSpec((None, 1, width), lambda i, pt: (i, 0, 0))
    out_shape = jax.ShapeDtypeStruct((n_steps, 1, width), BF16)
    scratch = [pltpu.VMEM((2, n_pages, page * N_HEADS, HEAD_W), F32),
               pltpu.VMEM((2, n_pages, page * N_HEADS, HEAD_W), F32),
               pltpu.VMEM((2, n_pages, N_HEADS, page), F32),
               pltpu.SemaphoreType.DMA((2, 3))]
    assert len(in_specs) == N_GUEST_IN and len(scratch) == N_GUEST_SCRATCH
    return in_specs, args, out_spec, out_shape, scratch


def _split_host_refs(pt_ref, refs, n_in, n_out, dec):
    if dec is None:
        return refs[:n_in], refs[n_in:n_in + n_out], None
    ins = refs[:n_in]
    g_in = refs[n_in:n_in + N_GUEST_IN]
    outs = refs[n_in + N_GUEST_IN:n_in + N_GUEST_IN + n_out]
    g_out = refs[n_in + N_GUEST_IN + n_out]
    g_scr = refs[n_in + N_GUEST_IN + n_out + 1:]
    layer, stride, offset = dec
    return ins, outs, _decode_guest(pt_ref, *g_in, g_out, *g_scr, layer=layer, stride=stride, offset=offset)


N_MERGE_IN = 17


def _merge_kernel(pt_ref, *refs, alpha, dec):
    ins, (o_ref,), guest = _split_host_refs(pt_ref, refs, N_MERGE_IN, 1, dec)
    (x_ref, shift_ref, scale_ref, gate_ref, oa_ref, ob_ref, oc_ref, sga_ref, sgb_ref, sgc_ref,
     wmg_ref, wbr_ref, wout_ref, gna_ref, gnb_ref, lng_ref, lnb_ref) = ins
    if guest is not None:
        guest[0]()
    x = x_ref[...]
    d = x.shape[1]
    u = (x * (1.0 + scale_ref[...]) + shift_ref[...]).astype(BF16)
    st = {"merged": jnp.zeros(x.shape, F32)}

    def head_rmsnorm(o, gain):
        parts = []
        for h in range(N_HEADS):
            blk = o[:, h * HEAD_W:(h + 1) * HEAD_W]
            parts.append(blk * lax.rsqrt(jnp.mean(blk * blk, axis=1, keepdims=True) + RMS_EPS))
        return jnp.concatenate(parts, axis=1) * gain

    def branch_input(n):
        def seg():
            if n == 0:
                br = head_rmsnorm(oa_ref[...].astype(F32), gna_ref[...]) * sga_ref[...].astype(F32)
            elif n == 1:
                br = head_rmsnorm(ob_ref[...].astype(F32), gnb_ref[...]) * sgb_ref[...].astype(F32)
            else:
                br = oc_ref[...].astype(F32) * sgc_ref[...].astype(F32)
            st["branch"] = br.astype(BF16)
        return seg

    def branch_gate(n):
        def seg():
            st["gate"] = _sigmoid(_dot(u, wmg_ref[:, n * d:(n + 1) * d]))
        return seg

    def branch_proj(n):
        def seg():
            st["merged"] = st["merged"] + st["gate"] * _dot(st["branch"], wbr_ref[n])
        return seg

    def out_proj():
        st["y"] = _dot(st["merged"].astype(BF16), wout_ref[...])

    def post_norm():
        r = alpha * x + gate_ref[...] * st["y"]
        mu = jnp.mean(r, axis=1, keepdims=True)
        rc = r - mu
        var = jnp.mean(rc * rc, axis=1, keepdims=True)
        o_ref[...] = rc * lax.rsqrt(var + LN_EPS) * lng_ref[...] + lnb_ref[...]

    segments = []
    for n in range(N_BRANCH):
        segments += [branch_input(n), branch_gate(n), branch_proj(n)]
    _run_with_guest(segments + [out_proj, post_norm], guest)


def _merge(page_table, x, shift, scale, gate, o_a, o_b, o_c, sga, sgb, sgc, wmg_all, wbr_all, wout_all,
           gna_l, gnb_l, lng_l, lnb_l, layer, rows_per_seq, alpha, dec=None):
    t, d = x.shape
    n_steps = None if dec is None else dec["qc"].shape[0] // dec["stride"]
    tm, mod_spec = _row_tiling(t, d, rows_per_seq, n_steps)
    row_spec = pl.BlockSpec((tm, d), lambda i, pt: (i, 0))
    br_spec = pl.BlockSpec((tm, _BW), lambda i, pt: (i, 0))
    in_specs = [row_spec, mod_spec, mod_spec, mod_spec] + [br_spec] * 6 + [
        _resident((None,) + wmg_all.shape[1:], lambda i, pt: (layer, 0, 0)),
        _resident((None,) + wbr_all.shape[1:], lambda i, pt: (layer, 0, 0, 0)),
        _resident((None,) + wout_all.shape[1:], lambda i, pt: (layer, 0, 0)),
        _resident(gna_l.shape, lambda i, pt: (0, 0)), _resident(gnb_l.shape, lambda i, pt: (0, 0)),
        _resident(lng_l.shape, lambda i, pt: (0, 0)), _resident(lnb_l.shape, lambda i, pt: (0, 0))]
    args = [x, shift, scale, gate, o_a, o_b, o_c, sga, sgb, sgc, wmg_all, wbr_all, wout_all, gna_l, gnb_l,
            lng_l, lnb_l]
    assert len(args) == N_MERGE_IN
    out_specs, out_shape, scratch = [row_spec], [jax.ShapeDtypeStruct((t, d), F32)], []
    if dec is not None:
        g_specs, g_args, g_out_spec, g_out_shape, scratch = _guest_plumbing(dec, t // tm)
        in_specs += g_specs
        args += g_args
        out_specs.append(g_out_spec)
        out_shape.append(g_out_shape)
    outs = pl.pallas_call(
        functools.partial(_merge_kernel, alpha=alpha,
                          dec=None if dec is None else (dec["layer"], dec["stride"], dec["offset"])),
        grid_spec=pltpu.PrefetchScalarGridSpec(
            num_scalar_prefetch=1, grid=(t // tm,), in_specs=in_specs, out_specs=out_specs,
            scratch_shapes=scratch),
        out_shape=out_shape,
        compiler_params=_params(("arbitrary",), 56), name="merge_out_projection",
    )(page_table, *args)
    return outs if dec is not None else outs[0]


def _permute_in_weights(w_in):
    widths = (N_HEADS * DK_GLA, N_HEADS * DK_GLA, _BW, _BW, GLA_RANK,
              _BW, _BW, _BW, _BW, _BW, _BW, _BW, N_HEADS, _BW)
    names = ("qa", "ka", "va", "ga", "ra", "qb", "fb", "ib", "gb", "qc", "kc", "vc", "fc", "gc")
    cols, off = {}, 0
    for name, w in zip(names, widths):
        cols[name] = w_in[:, :, off:off + w]
        off += w
    mg = w_in[:, :, off:]
    pad = jnp.zeros(w_in.shape[:2] + (LANES - N_HEADS - GLA_RANK,), w_in.dtype)
    order = [cols[n] for n in ("qa", "ka", "va", "ga", "qb", "fb", "ib", "gb", "qc", "kc", "vc", "gc")]
    order += [cols["fc"], cols["ra"], pad]
    return jnp.concatenate(order, axis=2).astype(BF16), mg.astype(BF16)


def kernel(x_prompt, x_sample, cache_fox_k, cache_fox_v, cache_fox_logf, state_gla, state_hgrn, page_table,
           c_prompt, c_sample, w_in, w_gla_a2, b_gla_a2, norm_gla, hgrn_lb_logits, norm_hgrn, b_fox_f,
           w_branch, w_out, w_ada, b_ada, ln_g, ln_b):
    batch, seq, d = x_prompt.shape
    n_dec = x_sample.shape[0]
    depth = w_in.shape[0]
    n_phys, page = cache_fox_k.shape[1], cache_fox_k.shape[2]
    alpha = (2 * depth) ** 0.25

    w_proj, w_mg = _permute_in_weights(w_in)
    wa2 = jnp.zeros((depth, LANES, N_HEADS * DK_GLA), F32).at[:, N_HEADS:N_HEADS + GLA_RANK, :].set(w_gla_a2)
    wa2 = wa2.astype(BF16)
    ba2 = b_gla_a2.reshape(depth, 1, -1)
    bf_pad = jnp.zeros((depth, 1, LANES), F32).at[:, 0, :N_HEADS].set(b_fox_f)
    w_br = w_branch.astype(BF16)
    w_o = w_out.astype(BF16)
    gna = norm_gla.reshape(depth, 1, -1)
    gnb = norm_hgrn.reshape(depth, 1, -1)
    lng = ln_g.reshape(depth, 1, -1)
    lnb = ln_b.reshape(depth, 1, -1)
    cache_k = cache_fox_k.reshape(depth, n_phys, page * N_HEADS, HEAD_W)
    cache_v = cache_fox_v.reshape(depth, n_phys, page * N_HEADS, HEAD_W)
    cache_lf = jnp.swapaxes(cache_fox_logf, 2, 3)

    lower = _lower_bounds(hgrn_lb_logits).reshape(depth, 1, -1)
    mod = _modulation(jnp.concatenate([c_prompt, c_sample], axis=0), w_ada, b_ada)

    xp = x_prompt.reshape(batch * seq, d)
    xs = x_sample.reshape(n_dec, d)
    outs = {k: [] for k in ("gla_p", "hg_p", "fp", "fs")}
    kp = jnp.zeros((depth, batch * seq * N_HEADS, HEAD_W), F32)
    vp = jnp.zeros_like(kp)
    ks = jnp.zeros((depth, n_dec * N_HEADS, HEAD_W), F32)
    vs = jnp.zeros_like(ks)
    gla_s = jnp.zeros_like(state_gla)
    hg_s = jnp.zeros_like(state_hgrn)
    for l in range(depth):
        mod_p = mod[l, :batch].reshape(batch, 1, 3 * d)
        shift_p, scale_p, gate_p = mod_p[:, :, :d], mod_p[:, :, d:2 * d], mod_p[:, :, 2 * d:]
        mod_s = mod[l, batch:]
        shift_s, scale_s, gate_s = mod_s[:, :d], mod_s[:, d:2 * d], mod_s[:, 2 * d:]

        zs = _inproj(page_table, xs, shift_s, scale_s, w_proj, wa2[l], ba2[l], lower[l], bf_pad[l], ks, vs, l, 1)
        ks, vs = zs["kc"], zs["vc"]
        o_as, gla_s = _scan_step(zs["qa"], zs["ka"], zs["va"], zs["ga"], state_gla, gla_s, l)
        o_bs, hg_s = _scan_step(zs["sqb"], zs["kb"], zs["ib"], zs["lf"], state_hgrn, hg_s, l)
        dec = dict(layer=l, stride=DECODE_HOSTS, n_pages=page_table.shape[1], qc=zs["qc"], kn_all=ks, vn_all=vs,
                   lfn=zs["zs"], cache_k=cache_k, cache_v=cache_v, cache_lf=cache_lf)

        z = _inproj(page_table, xp, shift_p, scale_p, w_proj, wa2[l], ba2[l], lower[l], bf_pad[l], kp, vp, l, seq,
                    dec=dict(dec, offset=0))
        kp, vp = z["kc"], z["vc"]
        o_a, s_a = _scan_prompt(z["qa"], z["ka"], z["va"], z["ga"], batch, seq, DK_GLA)
        o_b, s_b = _scan_prompt(z["sqb"], z["kb"], z["ib"], z["lf"], batch, seq, HEAD_W)
        cum_t = _forget_cumsum(z["zs"], batch, seq)
        o_c = _fox_attn_prompt(z["qc"], kp, vp, cum_t, l, batch, seq)
        xp, dec_odd = _merge(page_table, xp, shift_p, scale_p, gate_p, o_a, o_b, o_c, z["sga"], z["sgb"], z["sgc"],
                             w_mg, w_br, w_o, gna[l], gnb[l], lng[l], lnb[l], l, seq, alpha,
                             dec=dict(dec, offset=1))
        outs["gla_p"].append(s_a)
        outs["hg_p"].append(s_b)
        outs["fp"].append(z["logf"])

        o_cs = jnp.concatenate([z["dec"], dec_odd], axis=1).reshape(n_dec, N_HEADS * HEAD_W)
        xs = _merge(page_table, xs, shift_s, scale_s, gate_s, o_as, o_bs, o_cs, zs["sga"], zs["sgb"], zs["sgc"],
                    w_mg, w_br, w_o, gna[l], gnb[l], lng[l], lnb[l], l, 1, alpha)
        outs["fs"].append(zs["logf"])

    return (xp.reshape(batch, seq, d), xs.reshape(n_dec, 1, d),
            jnp.stack(outs["gla_p"]), gla_s, jnp.stack(outs["hg_p"]), hg_s,
            kp.reshape(depth, batch, seq, N_HEADS, HEAD_W), vp.reshape(depth, batch, seq, N_HEADS, HEAD_W),
            jnp.stack(outs["fp"]).reshape(depth, batch, seq, N_HEADS),
            ks.reshape(depth, n_dec, 1, N_HEADS, HEAD_W), vs.reshape(depth, n_dec, 1, N_HEADS, HEAD_W),
            jnp.stack(outs["fs"]).reshape(depth, n_dec, 1, N_HEADS))
```

```python
import functools

import jax
import jax.numpy as jnp
from jax import lax
from jax.experimental import pallas as pl
from jax.experimental.pallas import tpu as pltpu

F32 = jnp.float32
BF16 = jnp.bfloat16

N_HEADS = 4
DK_GLA = 64
HEAD_W = 128
GLA_RANK = 16
GLA_GATE_NORM = 16.0
N_BRANCH = 3
LN_EPS = 1e-5
RMS_EPS = 1e-6
F_FLOOR = 1e-30
MASK_VALUE = -1e30
LOG2E = 1.4426950408889634

LANES = 128
SUBLANES = 8

ROWS_INPROJ = 512
SCAN_CHUNK = 64
SCAN_ROWS = 1024
ATTN_BLOCK = 512
ATTN_Q_ROWS = 256
DECODE_ROWS = 8
DECODE_PAGE_GROUPS = 4
DECODE_HOSTS = 2
SCAN_SAFE_EXP = 80.0

_BW = N_HEADS * HEAD_W
_GROUPS = ("qka", "va", "ga", "qb", "fb", "ib", "gb", "qc", "kc", "vc", "gc", "small")
_WIDTHS = dict(qka=_BW, va=_BW, ga=_BW, qb=_BW, fb=_BW, ib=_BW, gb=_BW, qc=_BW, kc=_BW, vc=_BW,
               gc=_BW, small=LANES)
_OFFSETS = {}
_acc = 0
for _g in _GROUPS:
    _OFFSETS[_g] = _acc
    _acc += _WIDTHS[_g]
N_PROJ = _acc


def _dot(a, b):
    return jnp.dot(a, b, preferred_element_type=F32)


def _dot_nt(a, b):
    return lax.dot_general(a, b, (((1,), (1,)), ((), ())), preferred_element_type=F32)


def _dot_tn(a, b):
    return lax.dot_general(a, b, (((0,), (0,)), ((), ())), preferred_element_type=F32)


def _split3(x):
    x1 = x.astype(BF16)
    r1 = x - x1.astype(F32)
    x2 = r1.astype(BF16)
    x3 = (r1 - x2.astype(F32)).astype(BF16)
    return x1, x2, x3


def _dot_sel_lhs(sel, x):
    x1, x2, x3 = _split3(x)
    return (_dot(sel, x1) + _dot(sel, x2)) + _dot(sel, x3)


def _dot_sel_rhs(x, sel):
    x1, x2, x3 = _split3(x)
    return (_dot(x1, sel) + _dot(x2, sel)) + _dot(x3, sel)


def _log_sigmoid(x):
    return jnp.minimum(x, 0.0) - jnp.log1p(jnp.exp(-jnp.abs(x)))


def _sigmoid(x):
    return 1.0 / (1.0 + jnp.exp(-x))


def _silu(x):
    return x * _sigmoid(x)


def _lower_tri(n):
    r = lax.broadcasted_iota(jnp.int32, (n, n), 0)
    c = lax.broadcasted_iota(jnp.int32, (n, n), 1)
    return jnp.where(r >= c, 1.0, 0.0).astype(BF16)


def _params(semantics, vmem_mb):
    return pltpu.CompilerParams(dimension_semantics=semantics, vmem_limit_bytes=vmem_mb * 1024 * 1024)


def _resident(block_shape, index_map):
    return pl.BlockSpec(block_shape, index_map, pipeline_mode=pl.Buffered(1))


def _row_tiling(t, d, rows_per_seq, n_steps=None):
    if rows_per_seq == 1:
        tm = min(ROWS_INPROJ, t)
        return tm, pl.BlockSpec((tm, d), lambda i, pt: (i, 0))
    tm = min(ROWS_INPROJ, rows_per_seq) if n_steps is None else t // n_steps
    assert t % tm == 0 and rows_per_seq % tm == 0 and tm % SUBLANES == 0, (t, rows_per_seq, tm)
    tiles_per_seq = rows_per_seq // tm
    return tm, pl.BlockSpec((None, 1, d), lambda i, pt: (i // tiles_per_seq, 0, 0))


def _lower_bound_kernel(x_ref, o_ref):
    x = x_ref[...]
    depth = x.shape[0]
    e = jnp.exp(x - jnp.max(x, axis=0, keepdims=True))
    p = e / jnp.sum(e, axis=0, keepdims=True)
    acc = jnp.zeros_like(p[0:1])
    for l in range(depth):
        acc = acc + p[l:l + 1]
        o_ref[l:l + 1, :] = jnp.clip(acc - p[0:1], 0.0, 1.0)


def _lower_bounds(logits):
    return pl.pallas_call(
        _lower_bound_kernel, out_shape=jax.ShapeDtypeStruct(logits.shape, F32), name="hgrn_lower_bounds",
    )(logits.astype(F32))


def _mod_kernel(c_ref, w_ref, b_ref, o_ref):
    c = c_ref[...]
    o_ref[...] = _dot(_silu(c).astype(BF16), w_ref[...].astype(BF16)) + b_ref[...]


def _modulation(c_all, w_ada, b_ada):
    depth, d, n3 = w_ada.shape
    rows = c_all.shape[0]
    tn = 512
    return pl.pallas_call(
        _mod_kernel,
        grid=(depth, n3 // tn),
        in_specs=[pl.BlockSpec((rows, d), lambda l, j: (0, 0)),
                  pl.BlockSpec((None, d, tn), lambda l, j: (l, 0, j)),
                  pl.BlockSpec((None, 1, tn), lambda l, j: (l, 0, j))],
        out_specs=pl.BlockSpec((None, rows, tn), lambda l, j: (l, 0, j)),
        out_shape=jax.ShapeDtypeStruct((depth, rows, n3), F32),
        compiler_params=_params(("arbitrary", "arbitrary"), 32),
        name="adaln_modulation",
    )(c_all, w_ada, b_ada.reshape(depth, 1, n3))


def _run_with_guest(segments, guest):
    if guest is None:
        for seg in segments:
            seg()
        return
    _, stages, end = guest
    done = 0
    for i, seg in enumerate(segments):
        seg()
        while done < len(stages) and (done + 1) * len(segments) <= (i + 1) * len(stages):
            stages[done]()
            done += 1
    for stage in stages[done:]:
        stage()
    end()


N_INPROJ_IN = 10


def _inproj_kernel(pt_ref, *refs, dec):
    ins, outs, guest = _split_host_refs(pt_ref, refs, N_INPROJ_IN, len(_INPROJ_OUTS), dec)
    x_ref, shift_ref, scale_ref, w_ref, wa2_ref, ba2_ref, lb_ref, bf_ref, _, _ = ins
    (qa_ref, ka_ref, va_ref, sga_ref, ga_ref, sqb_ref, lf_ref, kb_ref, ib_ref, sgb_ref,
     qc_ref, kc_ref, vc_ref, sgc_ref, zs_ref, logf_ref) = outs
    if guest is not None:
        guest[0]()
    u = (x_ref[...] * (1.0 + scale_ref[...]) + shift_ref[...]).astype(BF16)
    tm = x_ref.shape[0]

    def proj(name):
        off = _OFFSETS[name]
        return _dot(u, w_ref[:, off:off + _WIDTHS[name]])

    def plain(name, ref, fn=None):
        def seg():
            z = proj(name)
            ref[...] = (z if fn is None else fn(z)).astype(ref.dtype)
        return seg

    def gla_qk():
        zqk = proj("qka")
        half = N_HEADS * DK_GLA
        qa_ref[...] = (zqk[:, :half] * (DK_GLA ** -0.5)).astype(qa_ref.dtype)
        ka_ref[...] = zqk[:, half:].astype(ka_ref.dtype)

    def small():
        zs = proj("small")
        pre = _dot(zs.astype(BF16), wa2_ref[...]) + ba2_ref[...]
        ga_ref[...] = _log_sigmoid(pre) * (1.0 / GLA_GATE_NORM)
        lane = lax.broadcasted_iota(jnp.int32, zs.shape, 1)
        logf = jnp.where(lane < N_HEADS, _log_sigmoid(zs + bf_ref[...]), 0.0)
        zs_ref[...] = logf
        logf_ref[...] = logf[:, :N_HEADS]

    def hgrn_forget():
        fb = proj("fb")
        lb = lb_ref[...]
        f = lb + (1.0 - lb) * _sigmoid(fb)
        lf_ref[...] = jnp.log(jnp.maximum(f, F_FLOOR))
        kb_ref[...] = ((1.0 - lb) * _sigmoid(-fb)).astype(kb_ref.dtype)

    def head_rows(name, ref):
        def seg():
            z = proj(name)
            for h in range(N_HEADS):
                ref[pl.ds(h, tm, stride=N_HEADS), :] = z[:, h * HEAD_W:(h + 1) * HEAD_W]
        return seg

    segments = [gla_qk, plain("va", va_ref), plain("ga", sga_ref, _silu), small,
                plain("qb", sqb_ref, _silu), hgrn_forget, plain("ib", ib_ref), plain("gb", sgb_ref, _silu),
                plain("qc", qc_ref, lambda z: z * (HEAD_W ** -0.5 * LOG2E)),
                head_rows("kc", kc_ref), head_rows("vc", vc_ref), plain("gc", sgc_ref, _silu)]
    _run_with_guest(segments, guest)


_INPROJ_OUTS = (
    ("qa", 1, N_HEADS * DK_GLA, BF16), ("ka", 1, N_HEADS * DK_GLA, BF16), ("va", 1, _BW, BF16),
    ("sga", 1, _BW, BF16), ("ga", 1, N_HEADS * DK_GLA, F32),
    ("sqb", 1, _BW, BF16), ("lf", 1, _BW, F32), ("kb", 1, _BW, BF16), ("ib", 1, _BW, BF16),
    ("sgb", 1, _BW, BF16),
    ("qc", 1, _BW, BF16), ("kc", N_HEADS, HEAD_W, F32), ("vc", N_HEADS, HEAD_W, F32), ("sgc", 1, _BW, BF16),
    ("zs", 1, LANES, F32), ("logf", 1, N_HEADS, F32))


def _inproj(page_table, x, shift, scale, w_all, wa2_l, ba2_l, lb_l, bf_l, kc_all, vc_all, layer, rows_per_seq,
            dec=None):
    t, d = x.shape
    n_steps = None if dec is None else dec["qc"].shape[0] // dec["stride"]
    tm, mod_spec = _row_tiling(t, d, rows_per_seq, n_steps)
    names = [name for name, _, _, _ in _INPROJ_OUTS]
    in_specs = [pl.BlockSpec((tm, d), lambda i, pt: (i, 0)), mod_spec, mod_spec,
                _resident((None, d, N_PROJ), lambda i, pt: (layer, 0, 0)),
                _resident(wa2_l.shape, lambda i, pt: (0, 0)),
                _resident(ba2_l.shape, lambda i, pt: (0, 0)),
                _resident(lb_l.shape, lambda i, pt: (0, 0)),
                _resident(bf_l.shape, lambda i, pt: (0, 0)),
                pl.BlockSpec(memory_space=pl.ANY), pl.BlockSpec(memory_space=pl.ANY)]
    args = [x, shift, scale, w_all, wa2_l, ba2_l, lb_l, bf_l, kc_all, vc_all]
    assert len(args) == N_INPROJ_IN
    out_specs, out_shape, scratch = [], [], []
    for name, r, w, dt in _INPROJ_OUTS:
        if name in ("kc", "vc"):
            out_specs.append(pl.BlockSpec((None, tm * r, w), lambda i, pt: (layer, i, 0)))
            out_shape.append(jax.ShapeDtypeStruct(kc_all.shape, dt))
        else:
            out_specs.append(pl.BlockSpec((tm * r, w), lambda i, pt: (i, 0)))
            out_shape.append(jax.ShapeDtypeStruct((t * r, w), dt))
    if dec is not None:
        g_specs, g_args, g_out_spec, g_out_shape, scratch = _guest_plumbing(dec, t // tm)
        in_specs += g_specs
        args += g_args
        out_specs.append(g_out_spec)
        out_shape.append(g_out_shape)
        names.append("dec")
    outs = pl.pallas_call(
        functools.partial(_inproj_kernel, dec=None if dec is None else (dec["layer"], dec["stride"], dec["offset"])),
        grid_spec=pltpu.PrefetchScalarGridSpec(
            num_scalar_prefetch=1, grid=(t // tm,), in_specs=in_specs, out_specs=out_specs,
            scratch_shapes=scratch),
        out_shape=out_shape,
        input_output_aliases={1 + 8: names.index("kc"), 1 + 9: names.index("vc")},
        compiler_params=_params(("arbitrary",), 56), name="in_projection",
    )(page_table, *args)
    return dict(zip(names, outs))


def _scan_kernel(q_ref, k_ref, v_ref, g_ref, o_ref, s_ref, st_scr, kf_scr, bf_scr, *, sub, chunk):
    step = pl.program_id(1)
    n_steps = pl.num_programs(1)
    n_chunks = q_ref.shape[0] // chunk
    n_units = q_ref.shape[1] // LANES
    dk = LANES // sub

    @pl.when(step == 0)
    def _():
        st_scr[...] = jnp.zeros_like(st_scr)

    tri = _lower_tri(chunk)
    row_cc = lax.broadcasted_iota(jnp.int32, (chunk, chunk), 0)
    col_cc = lax.broadcasted_iota(jnp.int32, (chunk, chunk), 1)
    causal = row_cc >= col_cc
    lane = lax.broadcasted_iota(jnp.int32, (chunk, LANES), 1)
    mid = chunk // 2

    cums, worst = [], jnp.zeros((1, 1), F32)
    for c in range(n_chunks):
        b = _dot_sel_lhs(tri, g_ref[c * chunk:(c + 1) * chunk, :])
        cums.append(b)
        worst = jnp.maximum(worst, jnp.max(jnp.abs(b - b[mid - 1:mid, :]), keepdims=True))
    fast = worst[0, 0] <= SCAN_SAFE_EXP

    def direct_scores(q, k_h, b):
        kf_scr[...] = k_h
        bf_scr[...] = b

        def body(s, acc):
            k_row = kf_scr[pl.ds(s, 1), :]
            b_row = bf_scr[pl.ds(s, 1), :]
            w = q * k_row * jnp.exp(jnp.minimum(b - b_row, 0.0))
            return jnp.where(col_cc == s, jnp.sum(w, axis=1, keepdims=True), acc)

        return lax.fori_loop(0, chunk, body, jnp.zeros((chunk, chunk), F32))

    def run(factorised):
        st = [st_scr[h] for h in range(N_HEADS)]
        for c in range(n_chunks):
            rs = slice(c * chunk, (c + 1) * chunk)
            for u in range(n_units):
                sl = slice(u * LANES, (u + 1) * LANES)
                b = cums[c][:, sl]
                b_end = b[chunk - 1:chunk, :]
                a = b - b[mid - 1:mid, :]
                q = q_ref[rs, sl].astype(F32)
                k = k_ref[rs, sl].astype(F32)
                q_in = (q * jnp.exp(b)).astype(BF16)
                k_out = k * jnp.exp(b_end - b)
                decay_end = jnp.exp(b_end)
                if factorised:
                    qs = (q * jnp.exp(a)).astype(BF16)
                    ks = k * jnp.exp(-a)
                for j in range(sub):
                    h = u * sub + j
                    hs = slice(h * HEAD_W, (h + 1) * HEAD_W)
                    in_head = None if sub == 1 else (lane // dk) == j

                    def own(x, in_head=in_head):
                        return x if in_head is None else jnp.where(in_head, x, 0.0)

                    if factorised:
                        scores = _dot_nt(qs, own(ks).astype(BF16))
                    else:
                        scores = direct_scores(q, own(k), b)
                    scores = jnp.where(causal, scores, 0.0)
                    v = v_ref[rs, hs]
                    o = _dot(scores.astype(BF16), v) + _dot_nt(q_in, st[h].astype(BF16))
                    o_ref[rs, hs] = o.astype(o_ref.dtype)
                    st[h] = st[h] * decay_end + _dot_tn(v, own(k_out).astype(BF16))
        for h in range(N_HEADS):
            st_scr[h] = st[h]

    @pl.when(fast)
    def _():
        run(True)

    @pl.when(jnp.logical_not(fast))
    def _():
        run(False)

    @pl.when(step == n_steps - 1)
    def _():
        for h in range(N_HEADS):
            j = h % sub
            s_full = jnp.transpose(st_scr[h])
            s_ref[h] = s_full[j * dk:(j + 1) * dk, :]


def _scan_prompt(q, k, v, g, batch, seq, dk):
    t = q.shape[0]
    sub = LANES // dk
    rows = min(SCAN_ROWS, seq)
    chunk = min(SCAN_CHUNK, rows)
    n_steps = seq // rows
    wq = q.shape[1]

    def at(b, c):
        return (b * n_steps + c, 0)

    o, s = pl.pallas_call(
        functools.partial(_scan_kernel, sub=sub, chunk=chunk),
        grid=(batch, n_steps),
        in_specs=[pl.BlockSpec((rows, wq), at), pl.BlockSpec((rows, wq), at),
                  pl.BlockSpec((rows, _BW), at), pl.BlockSpec((rows, wq), at)],
        out_specs=[pl.BlockSpec((rows, _BW), at),
                   pl.BlockSpec((None, N_HEADS, dk, HEAD_W), lambda b, c: (b, 0, 0, 0))],
        out_shape=[jax.ShapeDtypeStruct((t, _BW), BF16),
                   jax.ShapeDtypeStruct((batch, N_HEADS, dk, HEAD_W), F32)],
        scratch_shapes=[pltpu.VMEM((N_HEADS, HEAD_W, LANES), F32),
                        pltpu.VMEM((chunk, LANES), F32), pltpu.VMEM((chunk, LANES), F32)],
        compiler_params=_params(("arbitrary", "arbitrary"), 32),
        name="linear_scan_prompt",
    )(q, k, v, g)
    return o, s


def _scan_step_kernel(q_ref, k_ref, v_ref, g_ref, s_ref, so_all_ref, o_ref, so_ref):
    del so_all_ref
    rows = q_ref.shape[0]
    dk = s_ref.shape[2]
    for i in range(rows):
        for h in range(N_HEADS):
            ks = slice(h * dk, (h + 1) * dk)
            vs = slice(h * HEAD_W, (h + 1) * HEAD_W)
            decay = jnp.transpose(jnp.exp(g_ref[i:i + 1, ks]))
            k_col = jnp.transpose(k_ref[i:i + 1, ks].astype(F32))
            q_col = jnp.transpose(q_ref[i:i + 1, ks].astype(F32))
            v_row = v_ref[i:i + 1, vs].astype(F32)
            s_new = decay * s_ref[i, h] + k_col * v_row
            so_ref[i, h] = s_new
            o_ref[i:i + 1, vs] = jnp.sum(q_col * s_new, axis=0, keepdims=True).astype(o_ref.dtype)


def _scan_step(q, k, v, g, state_all, new_state_all, layer):
    _, n, _, dk, dv = state_all.shape
    rows = min(DECODE_ROWS, n)
    wq = q.shape[1]
    state_spec = pl.BlockSpec((None, rows, N_HEADS, dk, dv), lambda i: (layer, i, 0, 0, 0))
    return pl.pallas_call(
        _scan_step_kernel,
        grid=(n // rows,),
        in_specs=[pl.BlockSpec((rows, wq), lambda i: (i, 0)), pl.BlockSpec((rows, wq), lambda i: (i, 0)),
                  pl.BlockSpec((rows, _BW), lambda i: (i, 0)), pl.BlockSpec((rows, wq), lambda i: (i, 0)),
                  state_spec, pl.BlockSpec(memory_space=pl.ANY)],
        out_specs=[pl.BlockSpec((rows, _BW), lambda i: (i, 0)), state_spec],
        out_shape=[jax.ShapeDtypeStruct((n, _BW), BF16), jax.ShapeDtypeStruct(state_all.shape, F32)],
        input_output_aliases={5: 1},
        compiler_params=_params(("arbitrary",), 32),
        name="linear_scan_step",
    )(q, k, v, g, state_all, new_state_all)


def _forget_cumsum_kernel(z_ref, o_ref, carry_scr):
    @pl.when(pl.program_id(1) == 0)
    def _():
        carry_scr[...] = jnp.zeros_like(carry_scr)

    x = z_ref[...]
    n = x.shape[0]
    cum = _dot_sel_lhs(_lower_tri(n), x) + carry_scr[...]
    carry_scr[...] = cum[n - 1:n, :]
    o_ref[...] = jnp.transpose(cum * LOG2E)[:SUBLANES, :]


def _forget_cumsum(zs, batch, seq):
    blk = min(ATTN_BLOCK, seq)
    nb = seq // blk
    return pl.pallas_call(
        _forget_cumsum_kernel,
        grid=(batch, nb),
        in_specs=[pl.BlockSpec((blk, LANES), lambda b, c: (b * nb + c, 0))],
        out_specs=pl.BlockSpec((None, None, SUBLANES, blk), lambda b, c: (b, c, 0, 0)),
        out_shape=jax.ShapeDtypeStruct((batch, nb, SUBLANES, blk), F32),
        scratch_shapes=[pltpu.VMEM((1, LANES), F32)],
        compiler_params=_params(("arbitrary", "arbitrary"), 32),
        name="forget_cumsum",
    )(zs)


def _fox_attn_kernel(q_ref, k_ref, v_ref, cum_ref, o_ref, k_scr, v_scr):
    qi = pl.program_id(1)
    tq = q_ref.shape[0]
    tk = cum_ref.shape[2]
    seq = k_scr.shape[1]

    @pl.when(qi == 0)
    def _():
        for h in range(N_HEADS):
            k_scr[h] = k_ref[pl.ds(h, seq, stride=N_HEADS), :].astype(BF16)
            v_scr[h] = v_ref[pl.ds(h, seq, stride=N_HEADS), :].astype(BF16)

    q = [q_ref[:, h * HEAD_W:(h + 1) * HEAD_W] for h in range(N_HEADS)]
    n_full = (qi * tq) // tk

    def block(ki, carry, masked):
        rows = pl.ds(pl.multiple_of(ki * tk, tk), tk)
        out = []
        for h in range(N_HEADS):
            m, l, acc = carry[h]
            s = _dot_nt(q[h], k_scr[h, rows, :]) - cum_ref[ki, h:h + 1, :]
            if masked:
                r = lax.broadcasted_iota(jnp.int32, s.shape, 0) + qi * tq
                c = lax.broadcasted_iota(jnp.int32, s.shape, 1) + ki * tk
                s = jnp.where(r >= c, s, MASK_VALUE)
            m_new = jnp.maximum(m, jnp.max(s, axis=1, keepdims=True))
            p = jnp.exp2(s - m_new)
            alpha = jnp.exp2(m - m_new)
            l = alpha * l + jnp.sum(p, axis=1, keepdims=True)
            acc = alpha * acc + _dot(p.astype(BF16), v_scr[h, rows, :])
            out.append((m_new, l, acc))
        return tuple(out)

    init = tuple((jnp.full((tq, 1), MASK_VALUE, F32), jnp.zeros((tq, 1), F32), jnp.zeros((tq, HEAD_W), F32))
                 for _ in range(N_HEADS))
    carry = lax.fori_loop(0, n_full, lambda ki, c: block(ki, c, False), init)
    final = block(n_full, carry, True)
    for h in range(N_HEADS):
        _, l, acc = final[h]
        o_ref[:, h * HEAD_W:(h + 1) * HEAD_W] = (acc / l).astype(o_ref.dtype)


def _fox_attn_prompt(qc, kc_all, vc_all, cum_t, layer, batch, seq):
    t = qc.shape[0]
    blk = min(ATTN_BLOCK, seq)
    nb = seq // blk
    tq = min(ATTN_Q_ROWS, blk)
    nq = seq // tq
    kv_spec = pl.BlockSpec((None, seq * N_HEADS, HEAD_W), lambda b, i: (layer, b, 0))
    return pl.pallas_call(
        _fox_attn_kernel,
        grid=(batch, nq),
        in_specs=[pl.BlockSpec((tq, _BW), lambda b, i: (b * nq + i, 0)), kv_spec, kv_spec,
                  pl.BlockSpec((None, nb, SUBLANES, blk), lambda b, i: (b, 0, 0, 0))],
        out_specs=pl.BlockSpec((tq, _BW), lambda b, i: (b * nq + i, 0)),
        out_shape=jax.ShapeDtypeStruct((t, _BW), BF16),
        scratch_shapes=[pltpu.VMEM((N_HEADS, seq, HEAD_W), BF16), pltpu.VMEM((N_HEADS, seq, HEAD_W), BF16)],
        compiler_params=_params(("arbitrary", "arbitrary"), 40),
        name="fox_attention_prompt",
    )(qc, kc_all, vc_all, cum_t)


def _decode_guest(pt_ref, q_ref, kn_ref, vn_ref, lfn_ref, ck_hbm, cv_hbm, clf_hbm, o_ref,
                  k_buf, v_buf, lf_buf, sems, *, layer, stride, offset):
    b = pl.program_id(0)
    n_steps = pl.num_programs(0)
    n_pages = k_buf.shape[1]
    slot = b % 2

    def page_copies(step, slot):
        seq = step * stride + offset
        out = []
        for j in range(n_pages):
            pg = pt_ref[seq, j]
            out.append(pltpu.make_async_copy(ck_hbm.at[layer, pg], k_buf.at[slot, j], sems.at[slot, 0]))
            out.append(pltpu.make_async_copy(cv_hbm.at[layer, pg], v_buf.at[slot, j], sems.at[slot, 1]))
            out.append(pltpu.make_async_copy(clf_hbm.at[layer, pg], lf_buf.at[slot, j], sems.at[slot, 2]))
        return out

    def begin():
        @pl.when(b == 0)
        def _():
            for c in page_copies(0, 0):
                c.start()

        for c in page_copies(b, slot):
            c.wait()
        for c in page_copies(jnp.minimum(b + 1, n_steps - 1), 1 - slot):
            c.start()

    def end():
        @pl.when(b == n_steps - 1)
        def _():
            for c in page_copies(b, 1 - slot):
                c.wait()

    return begin, _decode_attend_stages(q_ref, kn_ref, vn_ref, lfn_ref, o_ref,
                                        k_buf.at[slot], v_buf.at[slot], lf_buf.at[slot]), end


def _decode_attend_stages(q_ref, kn_ref, vn_ref, lfn_ref, o_ref, k_buf, v_buf, lf_buf):
    n_pages = k_buf.shape[0]
    st = {}
    k_refs = [k_buf.at[j] for j in range(n_pages)]
    v_refs = [v_buf.at[j] for j in range(n_pages)]
    lf_refs = [lf_buf.at[j] for j in range(n_pages)]
    page = lf_buf.shape[2]
    wide = page * N_HEADS
    rows = n_pages * SUBLANES
    pad = jnp.zeros((SUBLANES - N_HEADS, HEAD_W), F32)

    def head_rows(x):
        return jnp.concatenate([x[:, h * HEAD_W:(h + 1) * HEAD_W] for h in range(N_HEADS)] + [pad], axis=0)

    def within_pages():
        lf_rows = jnp.concatenate([x for j in range(n_pages) for x in (lf_refs[j][...], pad)], axis=0)
        r_p = lax.broadcasted_iota(jnp.int32, (page, page), 0)
        c_p = lax.broadcasted_iota(jnp.int32, (page, page), 1)
        upto = jnp.where(r_p <= c_p, 1.0, 0.0).astype(BF16)
        st["cum_page"] = _dot_sel_rhs(lf_rows, upto)

    def across_pages():
        cum_page = st["cum_page"]
        totals = jnp.broadcast_to(cum_page[:, page - 1:page], (rows, page))
        r_r = lax.broadcasted_iota(jnp.int32, (rows, rows), 0)
        c_r = lax.broadcasted_iota(jnp.int32, (rows, rows), 1)
        earlier = jnp.where(((r_r % SUBLANES) == (c_r % SUBLANES)) & ((c_r // SUBLANES) < (r_r // SUBLANES)),
                            1.0, 0.0).astype(BF16)
        st["ck"] = cum_page + _dot_sel_lhs(earlier, totals)

    def widen():
        ck = st["ck"]
        lfn_col = jnp.transpose(lfn_ref[...])[:SUBLANES, :]
        ck_new = (ck[rows - SUBLANES:, page - 1:page] + lfn_col) * LOG2E
        r_e = lax.broadcasted_iota(jnp.int32, (page, wide), 0)
        c_e = lax.broadcasted_iota(jnp.int32, (page, wide), 1)
        spread = jnp.where((c_e // N_HEADS) == r_e, 1.0, 0.0).astype(BF16)
        st["ck_wide"] = _dot_sel_rhs(ck * LOG2E, spread)
        q4 = head_rows(q_ref[...].astype(F32))
        st["q4b"] = q4.astype(BF16)
        kn4 = jnp.concatenate([kn_ref[...], pad], axis=0)
        s_new = jnp.sum(q4 * kn4, axis=1, keepdims=True) - ck_new
        st["parts"] = [(s_new, jnp.ones_like(s_new), jnp.concatenate([vn_ref[...], pad], axis=0))]

    def logits_of(pages):
        def stage():
            row8 = lax.broadcasted_iota(jnp.int32, (SUBLANES, wide), 0)
            lane = lax.broadcasted_iota(jnp.int32, (SUBLANES, wide), 1)
            own_head = (lane % N_HEADS) == row8
            ck_wide, q4b = st["ck_wide"], st["q4b"]
            st["logits", pages[0]] = [jnp.where(
                own_head,
                _dot_nt(q4b, k_refs[j][...].astype(BF16)) - ck_wide[j * SUBLANES:(j + 1) * SUBLANES],
                MASK_VALUE) for j in pages]
        return stage

    def values_of(pages):
        def stage():
            logits = st.pop(("logits", pages[0]))
            m = functools.reduce(jnp.maximum, [jnp.max(s, axis=1, keepdims=True) for s in logits])
            l = jnp.zeros_like(m)
            acc = jnp.zeros((SUBLANES, HEAD_W), F32)
            for j, s in zip(pages, logits):
                p = jnp.exp2(s - m)
                l = l + jnp.sum(p, axis=1, keepdims=True)
                acc = acc + _dot(p.astype(BF16), v_refs[j][...].astype(BF16))
            st["parts"].append((m, l, acc))
        return stage

    def combine():
        parts = st["parts"]
        m_all = functools.reduce(jnp.maximum, [m for m, _, _ in parts])
        l_all = sum(jnp.exp2(m - m_all) * l for m, l, _ in parts)
        out = sum(jnp.exp2(m - m_all) * acc for m, _, acc in parts) / l_all
        o_ref[...] = jnp.concatenate([out[h:h + 1, :] for h in range(N_HEADS)], axis=1).astype(o_ref.dtype)

    per_group = pl.cdiv(n_pages, DECODE_PAGE_GROUPS)
    groups = [range(g0, min(g0 + per_group, n_pages)) for g0 in range(0, n_pages, per_group)]
    stages = [within_pages, across_pages, widen, logits_of(groups[0])]
    for prev, nxt in zip(groups[:-1], groups[1:]):
        stages += [logits_of(nxt), values_of(prev)]
    return stages + [values_of(groups[-1]), combine]


N_GUEST_IN = 7
N_GUEST_SCRATCH = 4


def _guest_plumbing(dec, n_steps):
    layer, stride, offset = dec["layer"], dec["stride"], dec["offset"]
    n, width = dec["qc"].shape
    depth = dec["kn_all"].shape[0]
    n_pages = dec["n_pages"]
    page = dec["cache_lf"].shape[3]
    assert n == n_steps * stride, (n, n_steps, stride)

    def seq(i):
        return i * stride + offset

    vec_spec = pl.BlockSpec((None, 1, width), lambda i, pt: (seq(i), 0, 0))
    new_spec = pl.BlockSpec((None, None, N_HEADS, HEAD_W), lambda i, pt: (layer, seq(i), 0, 0))
    hbm_spec = pl.BlockSpec(memory_space=pl.ANY)
    in_specs = [vec_spec, new_spec, new_spec, pl.BlockSpec((None, 1, LANES), lambda i, pt: (seq(i), 0, 0)),
                hbm_spec, hbm_spec, hbm_spec]
    args = [dec["qc"].reshape(n, 1, width), dec["kn_all"].reshape(depth, n, N_HEADS, HEAD_W),
            dec["vn_all"].reshape(depth, n, N_HEADS, HEAD_W), dec["lfn"].reshape(n, 1, LANES),
            dec["cache_k"], dec["cache_v"], dec["cache_lf"]]
    out_spec = pl.BlockSpec((None, 1, width), lambda i, pt: (i, 0, 0))
    out_shape = jax.ShapeDtypeStruct((n_steps, 1, width), BF16)
    scratch = [pltpu.VMEM((2, n_pages, page * N_HEADS, HEAD_W), F32),
               pltpu.VMEM((2, n_pages, page * N_HEADS, HEAD_W), F32),
               pltpu.VMEM((2, n_pages, N_HEADS, page), F32),
               pltpu.SemaphoreType.DMA((2, 3))]
    assert len(in_specs) == N_GUEST_IN and len(scratch) == N_GUEST_SCRATCH
    return in_specs, args, out_spec, out_shape, scratch


def _split_host_refs(pt_ref, refs, n_in, n_out, dec):
    if dec is None:
        return refs[:n_in], refs[n_in:n_in + n_out], None
    ins = refs[:n_in]
    g_in = refs[n_in:n_in + N_GUEST_IN]
    outs = refs[n_in + N_GUEST_IN:n_in + N_GUEST_IN + n_out]
    g_out = refs[n_in + N_GUEST_IN + n_out]
    g_scr = refs[n_in + N_GUEST_IN + n_out + 1:]
    layer, stride, offset = dec
    return ins, outs, _decode_guest(pt_ref, *g_in, g_out, *g_scr, layer=layer, stride=stride, offset=offset)


N_MERGE_IN = 17


def _merge_kernel(pt_ref, *refs, alpha, dec):
    ins, (o_ref,), guest = _split_host_refs(pt_ref, refs, N_MERGE_IN, 1, dec)
    (x_ref, shift_ref, scale_ref, gate_ref, oa_ref, ob_ref, oc_ref, sga_ref, sgb_ref, sgc_ref,
     wmg_ref, wbr_ref, wout_ref, gna_ref, gnb_ref, lng_ref, lnb_ref) = ins
    if guest is not None:
        guest[0]()
    x = x_ref[...]
    d = x.shape[1]
    u = (x * (1.0 + scale_ref[...]) + shift_ref[...]).astype(BF16)
    st = {"merged": jnp.zeros(x.shape, F32)}

    def head_rmsnorm(o, gain):
        parts = []
        for h in range(N_HEADS):
            blk = o[:, h * HEAD_W:(h + 1) * HEAD_W]
            parts.append(blk * lax.rsqrt(jnp.mean(blk * blk, axis=1, keepdims=True) + RMS_EPS))
        return jnp.concatenate(parts, axis=1) * gain

    def branch_input(n):
        def seg():
            if n == 0:
                br = head_rmsnorm(oa_ref[...].astype(F32), gna_ref[...]) * sga_ref[...].astype(F32)
            elif n == 1:
                br = head_rmsnorm(ob_ref[...].astype(F32), gnb_ref[...]) * sgb_ref[...].astype(F32)
            else:
                br = oc_ref[...].astype(F32) * sgc_ref[...].astype(F32)
            st["branch"] = br.astype(BF16)
        return seg

    def branch_gate(n):
        def seg():
            st["gate"] = _sigmoid(_dot(u, wmg_ref[:, n * d:(n + 1) * d]))
        return seg

    def branch_proj(n):
        def seg():
            st["merged"] = st["merged"] + st["gate"] * _dot(st["branch"], wbr_ref[n])
        return seg

    def out_proj():
        st["y"] = _dot(st["merged"].astype(BF16), wout_ref[...])

    def post_norm():
        r = alpha * x + gate_ref[...] * st["y"]
        mu = jnp.mean(r, axis=1, keepdims=True)
        rc = r - mu
        var = jnp.mean(rc * rc, axis=1, keepdims=True)
        o_ref[...] = rc * lax.rsqrt(var + LN_EPS) * lng_ref[...] + lnb_ref[...]

    segments = []
    for n in range(N_BRANCH):
        segments += [branch_input(n), branch_gate(n), branch_proj(n)]
    _run_with_guest(segments + [out_proj, post_norm], guest)


def _merge(page_table, x, shift, scale, gate, o_a, o_b, o_c, sga, sgb, sgc, wmg_all, wbr_all, wout_all,
           gna_l, gnb_l, lng_l, lnb_l, layer, rows_per_seq, alpha, dec=None):
    t, d = x.shape
    n_steps = None if dec is None else dec["qc"].shape[0] // dec["stride"]
    tm, mod_spec = _row_tiling(t, d, rows_per_seq, n_steps)
    row_spec = pl.BlockSpec((tm, d), lambda i, pt: (i, 0))
    br_spec = pl.BlockSpec((tm, _BW), lambda i, pt: (i, 0))
    in_specs = [row_spec, mod_spec, mod_spec, mod_spec] + [br_spec] * 6 + [
        _resident((None,) + wmg_all.shape[1:], lambda i, pt: (layer, 0, 0)),
        _resident((None,) + wbr_all.shape[1:], lambda i, pt: (layer, 0, 0, 0)),
        _resident((None,) + wout_all.shape[1:], lambda i, pt: (layer, 0, 0)),
        _resident(gna_l.shape, lambda i, pt: (0, 0)), _resident(gnb_l.shape, lambda i, pt: (0, 0)),
        _resident(lng_l.shape, lambda i, pt: (0, 0)), _resident(lnb_l.shape, lambda i, pt: (0, 0))]
    args = [x, shift, scale, gate, o_a, o_b, o_c, sga, sgb, sgc, wmg_all, wbr_all, wout_all, gna_l, gnb_l,
            lng_l, lnb_l]
    assert len(args) == N_MERGE_IN
    out_specs, out_shape, scratch = [row_spec], [jax.ShapeDtypeStruct((t, d), F32)], []
    if dec is not None:
        g_specs, g_args, g_out_spec, g_out_shape, scratch = _guest_plumbing(dec, t // tm)
        in_specs += g_specs
        args += g_args
        out_specs.append(g_out_spec)
        out_shape.append(g_out_shape)
    outs = pl.pallas_call(
        functools.partial(_merge_kernel, alpha=alpha,
                          dec=None if dec is None else (dec["layer"], dec["stride"], dec["offset"])),
        grid_spec=pltpu.PrefetchScalarGridSpec(
            num_scalar_prefetch=1, grid=(t // tm,), in_specs=in_specs, out_specs=out_specs,
            scratch_shapes=scratch),
        out_shape=out_shape,
        compiler_params=_params(("arbitrary",), 56), name="merge_out_projection",
    )(page_table, *args)
    return outs if dec is not None else outs[0]


def _permute_in_weights(w_in):
    widths = (N_HEADS * DK_GLA, N_HEADS * DK_GLA, _BW, _BW, GLA_RANK,
              _BW, _BW, _BW, _BW, _BW, _BW, _BW, N_HEADS, _BW)
    names = ("qa", "ka", "va", "ga", "ra", "qb", "fb", "ib", "gb", "qc", "kc", "vc", "fc", "gc")
    cols, off = {}, 0
    for name, w in zip(names, widths):
        cols[name] = w_in[:, :, off:off + w]
        off += w
    mg = w_in[:, :, off:]
    pad = jnp.zeros(w_in.shape[:2] + (LANES - N_HEADS - GLA_RANK,), w_in.dtype)
    order = [cols[n] for n in ("qa", "ka", "va", "ga", "qb", "fb", "ib", "gb", "qc", "kc", "vc", "gc")]
    order += [cols["fc"], cols["ra"], pad]
    return jnp.concatenate(order, axis=2).astype(BF16), mg.astype(BF16)


def kernel(x_prompt, x_sample, cache_fox_k, cache_fox_v, cache_fox_logf, state_gla, state_hgrn, page_table,
           c_prompt, c_sample, w_in, w_gla_a2, b_gla_a2, norm_gla, hgrn_lb_logits, norm_hgrn, b_fox_f,
           w_branch, w_out, w_ada, b_ada, ln_g, ln_b):
    batch, seq, d = x_prompt.shape
    n_dec = x_sample.shape[0]
    depth = w_in.shape[0]
    n_phys, page = cache_fox_k.shape[1], cache_fox_k.shape[2]
    alpha = (2 * depth) ** 0.25

    w_proj, w_mg = _permute_in_weights(w_in)
    wa2 = jnp.zeros((depth, LANES, N_HEADS * DK_GLA), F32).at[:, N_HEADS:N_HEADS + GLA_RANK, :].set(w_gla_a2)
    wa2 = wa2.astype(BF16)
    ba2 = b_gla_a2.reshape(depth, 1, -1)
    bf_pad = jnp.zeros((depth, 1, LANES), F32).at[:, 0, :N_HEADS].set(b_fox_f)
    w_br = w_branch.astype(BF16)
    w_o = w_out.astype(BF16)
    gna = norm_gla.reshape(depth, 1, -1)
    gnb = norm_hgrn.reshape(depth, 1, -1)
    lng = ln_g.reshape(depth, 1, -1)
    lnb = ln_b.reshape(depth, 1, -1)
    cache_k = cache_fox_k.reshape(depth, n_phys, page * N_HEADS, HEAD_W)
    cache_v = cache_fox_v.reshape(depth, n_phys, page * N_HEADS, HEAD_W)
    cache_lf = jnp.swapaxes(cache_fox_logf, 2, 3)

    lower = _lower_bounds(hgrn_lb_logits).reshape(depth, 1, -1)
    mod = _modulation(jnp.concatenate([c_prompt, c_sample], axis=0), w_ada, b_ada)

    xp = x_prompt.reshape(batch * seq, d)
    xs = x_sample.reshape(n_dec, d)
    outs = {k: [] for k in ("gla_p", "hg_p", "fp", "fs")}
    kp = jnp.zeros((depth, batch * seq * N_HEADS, HEAD_W), F32)
    vp = jnp.zeros_like(kp)
    ks = jnp.zeros((depth, n_dec * N_HEADS, HEAD_W), F32)
    vs = jnp.zeros_like(ks)
    gla_s = jnp.zeros_like(state_gla)
    hg_s = jnp.zeros_like(state_hgrn)
    for l in range(depth):
        mod_p = mod[l, :batch].reshape(batch, 1, 3 * d)
        shift_p, scale_p, gate_p = mod_p[:, :, :d], mod_p[:, :, d:2 * d], mod_p[:, :, 2 * d:]
        mod_s = mod[l, batch:]
        shift_s, scale_s, gate_s = mod_s[:, :d], mod_s[:, d:2 * d], mod_s[:, 2 * d:]

        zs = _inproj(page_table, xs, shift_s, scale_s, w_proj, wa2[l], ba2[l], lower[l], bf_pad[l], ks, vs, l, 1)
        ks, vs = zs["kc"], zs["vc"]
        o_as, gla_s = _scan_step(zs["qa"], zs["ka"], zs["va"], zs["ga"], state_gla, gla_s, l)
        o_bs, hg_s = _scan_step(zs["sqb"], zs["kb"], zs["ib"], zs["lf"], state_hgrn, hg_s, l)
        dec = dict(layer=l, stride=DECODE_HOSTS, n_pages=page_table.shape[1], qc=zs["qc"], kn_all=ks, vn_all=vs,
                   lfn=zs["zs"], cache_k=cache_k, cache_v=cache_v, cache_lf=cache_lf)

        z = _inproj(page_table, xp, shift_p, scale_p, w_proj, wa2[l], ba2[l], lower[l], bf_pad[l], kp, vp, l, seq,
                    dec=dict(dec, offset=0))
        kp, vp = z["kc"], z["vc"]
        o_a, s_a = _scan_prompt(z["qa"], z["ka"], z["va"], z["ga"], batch, seq, DK_GLA)
        o_b, s_b = _scan_prompt(z["sqb"], z["kb"], z["ib"], z["lf"], batch, seq, HEAD_W)
        cum_t = _forget_cumsum(z["zs"], batch, seq)
        o_c = _fox_attn_prompt(z["qc"], kp, vp, cum_t, l, batch, seq)
        xp, dec_odd = _merge(page_table, xp, shift_p, scale_p, gate_p, o_a, o_b, o_c, z["sga"], z["sgb"], z["sgc"],
                             w_mg, w_br, w_o, gna[l], gnb[l], lng[l], lnb[l], l, seq, alpha,
                             dec=dict(dec, offset=1))
        outs["gla_p"].append(s_a)
        outs["hg_p"].append(s_b)
        outs["fp"].append(z["logf"])

        o_cs = jnp.concatenate([z["dec"], dec_odd], axis=1).reshape(n_dec, N_HEADS * HEAD_W)
        xs = _merge(page_table, xs, shift_s, scale_s, gate_s, o_as, o_bs, o_cs, zs["sga"], zs["sgb"], zs["sgc"],
                    w_mg, w_br, w_o, gna[l], gnb[l], lng[l], lnb[l], l, 1, alpha)
        outs["fs"].append(zs["logf"])

    return (xp.reshape(batch, seq, d), xs.reshape(n_dec, 1, d),
            jnp.stack(outs["gla_p"]), gla_s, jnp.stack(outs["hg_p"]), hg_s,
            kp.reshape(depth, batch, seq, N_HEADS, HEAD_W), vp.reshape(depth, batch, seq, N_HEADS, HEAD_W),
            jnp.stack(outs["fp"]).reshape(depth, batch, seq, N_HEADS),
            ks.reshape(depth, n_dec, 1, N_HEADS, HEAD_W), vs.reshape(depth, n_dec, 1, N_HEADS, HEAD_W),
            jnp.stack(outs["fs"]).reshape(depth, n_dec, 1, N_HEADS))
```
